```python
import math
import jax, jax.numpy as jnp
from jax import lax
import numpy as np

D_MODEL = 1024
BATCH = 8
SEQ = 2048
DEPTH = 1

GLA_HEADS = 4
GLA_DK = D_MODEL // 2
GLA_DV = D_MODEL
GLA_HK = GLA_DK // GLA_HEADS
GLA_HV = GLA_DV // GLA_HEADS
GLA_RANK = 16
GLA_TAU = 16.0
GLA_CHUNK = 64
SB_HEADS = 8
SB_WIDTH = D_MODEL
SB_HD = SB_WIDTH // SB_HEADS
SB_BLOCK = 128
N_BRANCH = 2
EPS = 1e-6

SPLIT_SIZES = [GLA_DK, GLA_DK, GLA_DV, GLA_DV, GLA_RANK,
               SB_WIDTH, SB_WIDTH, SB_WIDTH, SB_WIDTH, N_BRANCH * D_MODEL]
SPLIT_IDX = [int(v) for v in np.cumsum(SPLIT_SIZES)[:-1]]
IN_COLS = int(sum(SPLIT_SIZES))

kernel_name = "gla_stickbreaking_gated_hybrid"


def rmsnorm(x, g):
    xf = x.astype(jnp.float32)
    y = xf * lax.rsqrt(jnp.mean(xf * xf, axis=-1, keepdims=True) + EPS)
    return y.astype(x.dtype) * g


def gla_chunked(q, k, v, log_a):
    B, T, H, dk = q.shape
    dv = v.shape[-1]
    C = GLA_CHUNK
    n = T // C
    def to_chunks(a):
        return a.reshape(B, n, C, H, a.shape[-1]).transpose(0, 3, 1, 2, 4).astype(jnp.float32)
    qc, kc, vc, la = to_chunks(q), to_chunks(k), to_chunks(v), to_chunks(log_a)
    bcum = jnp.cumsum(la, axis=3)
    b_last = bcum[:, :, :, -1:, :]
    qe = qc * jnp.exp(bcum) * (dk ** -0.5)
    ke = kc * jnp.exp(-bcum)
    kd = kc * jnp.exp(b_last - bcum)
    mask = jnp.tril(jnp.ones((C, C), jnp.float32))
    attn = jnp.einsum('bhnid,bhnjd->bhnij', qe, ke) * mask
    o_intra = jnp.einsum('bhnij,bhnjv->bhniv', attn, vc)

    def step(S, inp):
        q_i, k_i, v_i, d_i = inp
        o = jnp.einsum('bhid,bhdv->bhiv', q_i, S)
        S = d_i[..., None] * S + jnp.einsum('bhjd,bhjv->bhdv', k_i, v_i)
        return S, o

    xs = (jnp.moveaxis(qe, 2, 0), jnp.moveaxis(kd, 2, 0), jnp.moveaxis(vc, 2, 0),
          jnp.moveaxis(jnp.exp(b_last[:, :, :, 0, :]), 2, 0))
    S0 = jnp.zeros((B, H, dk, dv), jnp.float32)
    _, o_inter = lax.scan(step, S0, xs)
    o = o_intra + jnp.moveaxis(o_inter, 0, 2)
    return o.transpose(0, 2, 3, 1, 4).reshape(B, T, H, dv)


def stick_breaking(q, k, v):
    B, T, H, d = q.shape
    qh = q.transpose(0, 2, 1, 3)
    kh = k.transpose(0, 2, 1, 3)
    vh = v.transpose(0, 2, 1, 3)
    scale = 1.0 / math.sqrt(d)
    outs = []
    for i in range(T // SB_BLOCK):
        L = (i + 1) * SB_BLOCK
        qb = qh[:, :, i * SB_BLOCK:L]
        z = jnp.einsum('bhqd,bhkd->bhqk', qb, kh[:, :, :L]).astype(jnp.float32) * scale
        tpos = i * SB_BLOCK + jnp.arange(SB_BLOCK)[:, None]
        spos = jnp.arange(L)[None, :]
        causal = spos < tpos
        log1m = jnp.where(causal, jax.nn.log_sigmoid(-z), 0.0)
        between = lax.cumsum(log1m, axis=3, reverse=True) - log1m
        A = jnp.where(causal, jnp.exp(jax.nn.log_sigmoid(z) + between), 0.0)
        outs.append(jnp.einsum('bhqk,bhkd->bhqd', A.astype(vh.dtype), vh[:, :, :L]))
    o = jnp.concatenate(outs, axis=2)
    return o.transpose(0, 2, 1, 3)


def setup_inputs(seed: int = 0) -> dict:
    key = jax.random.key(seed)
    ks = jax.random.split(key, 12)
    f = jnp.float32
    x = jax.random.normal(ks[0], (BATCH, SEQ, D_MODEL), f)
    norm_g = 1.0 + 0.02 * jax.random.normal(ks[1], (D_MODEL,), f)
    w_in = jax.random.normal(ks[2], (D_MODEL, IN_COLS), f) * D_MODEL ** -0.5
    w_dec_up = jax.random.normal(ks[3], (GLA_RANK, GLA_DK), f) * GLA_RANK ** -0.5
    b_dec = 0.1 * jax.random.normal(ks[4], (GLA_DK,), f)
    gla_norm_g = 1.0 + 0.02 * jax.random.normal(ks[5], (GLA_HV,), f)
    w_pa = jax.random.normal(ks[6], (GLA_DV, D_MODEL), f) * GLA_DV ** -0.5
    w_pb = jax.random.normal(ks[7], (SB_WIDTH, D_MODEL), f) * SB_WIDTH ** -0.5
    b_gate = 0.01 * jax.random.normal(ks[8], (N_BRANCH * D_MODEL,), f)
    w_o = jax.random.normal(ks[9], (D_MODEL, D_MODEL), f) * D_MODEL ** -0.5
    final_g = 1.0 + 0.02 * jax.random.normal(ks[10], (D_MODEL,), f)
    return {"x": x, "norm_g": norm_g, "w_in": w_in, "w_dec_up": w_dec_up, "b_dec": b_dec,
            "gla_norm_g": gla_norm_g, "w_pa": w_pa, "w_pb": w_pb, "b_gate": b_gate,
            "w_o": w_o, "final_g": final_g}


def reference(x, norm_g, w_in, w_dec_up, b_dec, gla_norm_g, w_pa, w_pb, b_gate, w_o, final_g):
    B, T, _ = x.shape
    for _layer in range(DEPTH):
        h = rmsnorm(x, norm_g)
        proj = h @ w_in
        (g_q, g_k, g_v, g_gate, g_rank,
         s_q, s_k, s_v, s_gate, m_logits) = jnp.split(proj, SPLIT_IDX, axis=-1)

        log_a = jax.nn.log_sigmoid((g_rank @ w_dec_up + b_dec).astype(jnp.float32)) / GLA_TAU
        o_gla = gla_chunked(g_q.reshape(B, T, GLA_HEADS, GLA_HK),
                            g_k.reshape(B, T, GLA_HEADS, GLA_HK),
                            g_v.reshape(B, T, GLA_HEADS, GLA_HV),
                            log_a.reshape(B, T, GLA_HEADS, GLA_HK)).astype(x.dtype)
        o_gla = rmsnorm(o_gla, gla_norm_g).reshape(B, T, GLA_DV) * jax.nn.silu(g_gate)
        y_a = o_gla @ w_pa

        o_sb = stick_breaking(s_q.reshape(B, T, SB_HEADS, SB_HD),
                              s_k.reshape(B, T, SB_HEADS, SB_HD),
                              s_v.reshape(B, T, SB_HEADS, SB_HD))
        o_sb = o_sb.reshape(B, T, SB_WIDTH) * jax.nn.silu(s_gate)
        y_b = o_sb @ w_pb

        gates = jax.nn.sigmoid(m_logits + b_gate).reshape(B, T, N_BRANCH, D_MODEL)
        merged = gates[:, :, 0] * y_a + gates[:, :, 1] * y_b
        x = x + merged @ w_o
    return rmsnorm(x, final_g)
```

```python
import functools

import jax
import jax.numpy as jnp
from jax import lax
from jax.experimental import pallas as pl
from jax.experimental.pallas import tpu as pltpu

D_MODEL = 1024
GLA_HEADS = 4
GLA_HK = 128
GLA_HV = 256
GLA_DK = GLA_HEADS * GLA_HK
GLA_DV = GLA_HEADS * GLA_HV
GLA_RANK = 16
GLA_TAU = 16.0
GLA_CHUNK = 64
SB_HEADS = 8
SB_HD = 128
SB_WIDTH = SB_HEADS * SB_HD
EPS = 1e-6

LANES = 128
RANK_PAD = LANES
VMEM_LIMIT = 56 * 1024 * 1024

F32 = jnp.float32
BF16 = jnp.bfloat16

_NT = (((1,), (1,)), ((), ()))
_TN = (((0,), (0,)), ((), ()))


def _sigmoid(x):
    return 1.0 / (1.0 + jnp.exp(-x))


def _softplus(x):
    return jnp.maximum(x, 0.0) + jnp.log1p(jnp.exp(-jnp.abs(x)))


def _inproj_kernel(x_ref, ng_ref, wgq, wgk, wgv, wgg, wgr, wsq, wsk, wsv, wsg, wm, bg_ref,
                   gq_o, gk_o, gv_o, gg_o, gr_o, sq_o, sk_o, sv_o, sg_o, m_o):
    x = x_ref[...]
    h = x * lax.rsqrt(jnp.mean(x * x, axis=-1, keepdims=True) + EPS) * ng_ref[...]
    hb = h.astype(BF16)

    def proj(w_ref):
        return jnp.dot(hb, w_ref[...], preferred_element_type=F32)

    gq_o[...] = (proj(wgq) * (GLA_HK ** -0.5)).astype(BF16)
    gk_o[...] = proj(wgk).astype(BF16)
    gv_o[...] = proj(wgv).astype(BF16)
    g = proj(wgg)
    gg_o[...] = (g * _sigmoid(g)).astype(BF16)
    gr_o[...] = proj(wgr).astype(BF16)
    sq_o[...] = (proj(wsq) * (SB_HD ** -0.5)).astype(BF16)
    sk_o[...] = proj(wsk).astype(BF16)
    sv_o[...] = proj(wsv).astype(BF16)
    g = proj(wsg)
    sg_o[...] = (g * _sigmoid(g)).astype(BF16)
    m_o[...] = _sigmoid(proj(wm) + bg_ref[...]).astype(BF16)


def _inproj(x2, norm_g, ws, b_gate, tm):
    m = x2.shape[0]
    widths = [w.shape[1] for w in ws]
    const = lambda i: (0, 0)
    w_specs = [pl.BlockSpec((D_MODEL, n), const, pipeline_mode=pl.Buffered(1)) for n in widths]
    in_specs = ([pl.BlockSpec((tm, D_MODEL), lambda i: (i, 0)),
                 pl.BlockSpec((1, D_MODEL), const)]
                + w_specs
                + [pl.BlockSpec((1, 2 * D_MODEL), const)])
    out_specs = [pl.BlockSpec((tm, n), lambda i: (i, 0)) for n in widths]
    out_shape = [jax.ShapeDtypeStruct((m, n), BF16) for n in widths]
    return pl.pallas_call(
        _inproj_kernel,
        grid=(m // tm,),
        in_specs=in_specs,
        out_specs=out_specs,
        out_shape=out_shape,
        compiler_params=pltpu.CompilerParams(
            dimension_semantics=("parallel",), vmem_limit_bytes=VMEM_LIMIT),
        name="inproj",
    )(x2, norm_g, *ws, b_gate)


def _gla_kernel(q_ref, k_ref, v_ref, gate_ref, r_ref, wdu_ref, bdec_ref, gng_ref, o_ref, st_ref,
                *, seq):
    c = GLA_CHUNK
    st_ref[...] = jnp.zeros_like(st_ref)
    row = lax.broadcasted_iota(jnp.int32, (c, c), 0)
    col = lax.broadcasted_iota(jnp.int32, (c, c), 1)
    tril = row >= col
    tril_b = jnp.where(tril, 1.0, 0.0).astype(BF16)

    def body(ci, carry):
        r0 = pl.multiple_of(ci * c, c)
        rows = pl.ds(r0, c)
        u = jnp.dot(r_ref[rows, :], wdu_ref[...], preferred_element_type=F32) + bdec_ref[...]
        la = -_softplus(-u) * (1.0 / GLA_TAU)
        la_hi = la.astype(BF16)
        la_lo = (la - la_hi.astype(F32)).astype(BF16)
        bcum = (jnp.dot(tril_b, la_hi, preferred_element_type=F32)
                + jnp.dot(tril_b, la_lo, preferred_element_type=F32))
        blast = bcum[c - 1:c, :]
        q = q_ref[rows, :].astype(F32)
        k = k_ref[rows, :].astype(F32)
        qe = (q * jnp.exp(bcum)).astype(BF16)
        ke = (k * jnp.exp(-bcum)).astype(BF16)
        kd = (k * jnp.exp(blast - bcum)).astype(BF16)
        dlast = jnp.exp(blast)
        for h in range(GLA_HEADS):
            ks = slice(h * GLA_HK, (h + 1) * GLA_HK)
            vs = slice(h * GLA_HV, (h + 1) * GLA_HV)
            v = v_ref[rows, vs]
            attn = lax.dot_general(qe[:, ks], ke[:, ks], _NT, preferred_element_type=F32)
            attn = jnp.where(tril, attn, 0.0).astype(BF16)
            st = st_ref[h]
            o = (jnp.dot(attn, v, preferred_element_type=F32)
                 + lax.dot_general(qe[:, ks], st.astype(BF16), _NT, preferred_element_type=F32))
            st_ref[h] = st * dlast[:, ks] + lax.dot_general(
                v, kd[:, ks], _TN, preferred_element_type=F32)
            on = o * lax.rsqrt(jnp.mean(o * o, axis=-1, keepdims=True) + EPS) * gng_ref[...]
            o_ref[rows, vs] = (on * gate_ref[rows, vs].astype(F32)).astype(BF16)
        return carry

    lax.fori_loop(0, seq // c, body, 0)


def _gla(gq, gk, gv, gg, gr, wdu, b_dec, gla_norm_g, batch, seq):
    const = lambda b: (0, 0)
    per_b = lambda n: pl.BlockSpec((seq, n), lambda b: (b, 0))
    return pl.pallas_call(
        functools.partial(_gla_kernel, seq=seq),
        grid=(batch,),
        in_specs=[per_b(GLA_DK), per_b(GLA_DK), per_b(GLA_DV), per_b(GLA_DV), per_b(RANK_PAD),
                  pl.BlockSpec((RANK_PAD, GLA_DK), const),
                  pl.BlockSpec((1, GLA_DK), const),
                  pl.BlockSpec((1, GLA_HV), const)],
        out_specs=per_b(GLA_DV),
        out_shape=jax.ShapeDtypeStruct((batch * seq, GLA_DV), BF16),
        scratch_shapes=[pltpu.VMEM((GLA_HEADS, GLA_HV, GLA_HK), F32)],
        compiler_params=pltpu.CompilerParams(
            dimension_semantics=("parallel",), vmem_limit_bytes=VMEM_LIMIT),
        name="gla",
    )(gq, gk, gv, gg, gr, wdu, b_dec, gla_norm_g)


def _sb_kernel(q_ref, k_ref, v_ref, g_ref, o_ref, *, tq, tk):
    i = pl.program_id(2)
    t0 = i * tq
    kb_last = t0 // tk
    q = q_ref[...]
    row = lax.broadcasted_iota(jnp.int32, (tq, tk), 0)
    col = lax.broadcasted_iota(jnp.int32, (tq, tk), 1)
    jrow = lax.broadcasted_iota(jnp.int32, (tk, tk), 0)
    scol = lax.broadcasted_iota(jnp.int32, (tk, tk), 1)
    tri_b = jnp.where(jrow > scol, 1.0, 0.0).astype(BF16)

    def block(kb, carry, acc, masked):
        k0 = pl.multiple_of(kb * tk, tk)
        z = lax.dot_general(q, k_ref[pl.ds(k0, tk), :], _NT, preferred_element_type=F32)
        l1 = -_softplus(z)
        if masked:
            causal = (k0 + col) < (t0 + row)
            l1 = jnp.where(causal, l1, 0.0)
        between = jnp.dot(l1.astype(BF16), tri_b, preferred_element_type=F32)
        a = jnp.exp(z + l1 + between + carry)
        if masked:
            a = jnp.where(causal, a, 0.0)
        acc = acc + jnp.dot(a.astype(BF16), v_ref[pl.ds(k0, tk), :], preferred_element_type=F32)
        carry = carry + jnp.sum(l1, axis=1, keepdims=True)
        return carry, acc

    carry, acc = block(kb_last, jnp.zeros((tq, 1), F32), jnp.zeros((tq, SB_HD), F32), True)

    def body(j, ca):
        return block(kb_last - 1 - j, ca[0], ca[1], False)

    carry, acc = lax.fori_loop(0, kb_last, body, (carry, acc))
    o_ref[...] = (acc * g_ref[...].astype(F32)).astype(BF16)


def _sb(sq, sk, sv, sg, batch, seq, tq, tk):
    nq = seq // tq
    qspec = pl.BlockSpec((tq, SB_HD), lambda b, h, i: (b * nq + i, h))
    kvspec = pl.BlockSpec((seq, SB_HD), lambda b, h, i: (b, h))
    return pl.pallas_call(
        functools.partial(_sb_kernel, tq=tq, tk=tk),
        grid=(batch, SB_HEADS, nq),
        in_specs=[qspec, kvspec, kvspec, qspec],
        out_specs=qspec,
        out_shape=jax.ShapeDtypeStruct((batch * seq, SB_WIDTH), BF16),
        compiler_params=pltpu.CompilerParams(
            dimension_semantics=("parallel", "parallel", "parallel"),
            vmem_limit_bytes=VMEM_LIMIT),
        name="stickbreak",
    )(sq, sk, sv, sg)


def _out_kernel(oa_ref, ob_ref, m_ref, x_ref, wpa_ref, wpb_ref, wo_ref, fg_ref, o_ref):
    ya = jnp.dot(oa_ref[...], wpa_ref[...], preferred_element_type=F32)
    yb = jnp.dot(ob_ref[...], wpb_ref[...], preferred_element_type=F32)
    merged = (m_ref[:, :D_MODEL].astype(F32) * ya + m_ref[:, D_MODEL:].astype(F32) * yb)
    y = x_ref[...] + jnp.dot(merged.astype(BF16), wo_ref[...], preferred_element_type=F32)
    o_ref[...] = y * lax.rsqrt(jnp.mean(y * y, axis=-1, keepdims=True) + EPS) * fg_ref[...]


def _out(oa, ob, gates, x2, wpa, wpb, wo, final_g, tm):
    m = x2.shape[0]
    const = lambda i: (0, 0)
    rows = lambda n: pl.BlockSpec((tm, n), lambda i: (i, 0))
    wspec = pl.BlockSpec((D_MODEL, D_MODEL), const, pipeline_mode=pl.Buffered(1))
    return pl.pallas_call(
        _out_kernel,
        grid=(m // tm,),
        in_specs=[rows(GLA_DV), rows(SB_WIDTH), rows(2 * D_MODEL), rows(D_MODEL),
                  wspec, wspec, wspec, pl.BlockSpec((1, D_MODEL), const)],
        out_specs=rows(D_MODEL),
        out_shape=jax.ShapeDtypeStruct((m, D_MODEL), F32),
        compiler_params=pltpu.CompilerParams(
            dimension_semantics=("parallel",), vmem_limit_bytes=VMEM_LIMIT),
        name="outproj",
    )(oa, ob, gates, x2, wpa, wpb, wo, final_g)


def kernel(x, norm_g, w_in, w_dec_up, b_dec, gla_norm_g, w_pa, w_pb, b_gate, w_o, final_g):
    batch, seq, _ = x.shape
    x2 = x.reshape(batch * seq, D_MODEL)

    sizes = [GLA_DK, GLA_DK, GLA_DV, GLA_DV, GLA_RANK,
             SB_WIDTH, SB_WIDTH, SB_WIDTH, SB_WIDTH, 2 * D_MODEL]
    ws, off = [], 0
    for n in sizes:
        ws.append(w_in[:, off:off + n].astype(BF16))
        off += n
    ws[4] = jnp.pad(ws[4], ((0, 0), (0, RANK_PAD - GLA_RANK)))
    wdu = jnp.pad(w_dec_up, ((0, RANK_PAD - GLA_RANK), (0, 0))).astype(BF16)

    gq, gk, gv, gg, gr, sq, sk, sv, sg, gates = _inproj(
        x2, norm_g.reshape(1, D_MODEL), ws, b_gate.reshape(1, 2 * D_MODEL), tm=256)

    oa = _gla(gq, gk, gv, gg, gr, wdu, b_dec.reshape(1, GLA_DK),
              gla_norm_g.reshape(1, GLA_HV), batch, seq)
    ob = _sb(sq, sk, sv, sg, batch, seq, tq=128, tk=256)

    out = _out(oa, ob, gates, x2, w_pa.astype(BF16), w_pb.astype(BF16), w_o.astype(BF16),
               final_g.reshape(1, D_MODEL), tm=512)
    return out.reshape(batch, seq, D_MODEL)
```

```python
import functools

import numpy as np
import jax
import jax.numpy as jnp
from jax import lax
from jax.experimental import pallas as pl
from jax.experimental.pallas import tpu as pltpu

D_MODEL = 1024
GLA_HEADS = 4
GLA_HK = 128
GLA_HV = 256
GLA_DK = GLA_HEADS * GLA_HK
GLA_DV = GLA_HEADS * GLA_HV
GLA_RANK = 16
GLA_TAU = 16.0
GLA_CHUNK = 64
SB_HEADS = 8
SB_HD = 128
SB_WIDTH = SB_HEADS * SB_HD
EPS = 1e-6
LOG2E = 1.4426950408889634
SB_STAGES = 3
MASK_BIAS = -1e30

LANES = 128
RANK_PAD = LANES
VMEM_LIMIT = 56 * 1024 * 1024

F32 = jnp.float32
BF16 = jnp.bfloat16

_NT = (((1,), (1,)), ((), ()))
_TN = (((0,), (0,)), ((), ()))


def _sigmoid(x):
    return 1.0 / (1.0 + jnp.exp(-x))


def _softplus(x):
    return jnp.maximum(x, 0.0) + jnp.log1p(jnp.exp(-jnp.abs(x)))


def _inproj_kernel(x_ref, ng_ref, wgq, wgk, wgv, wgg, wgr, wsq, wsk, wsv, wsg, wm, bg_ref,
                   gq_o, gk_o, gv_o, gg_o, gr_o, sq_o, sk_o, sv_o, sg_o, m_o):
    x = x_ref[...]
    h = x * lax.rsqrt(jnp.mean(x * x, axis=-1, keepdims=True) + EPS) * ng_ref[...]
    hb = h.astype(BF16)

    def proj(w_ref):
        return jnp.dot(hb, w_ref[...], preferred_element_type=F32)

    gq_o[...] = (proj(wgq) * (GLA_HK ** -0.5)).astype(BF16)
    gk_o[...] = proj(wgk).astype(BF16)
    gv_o[...] = proj(wgv).astype(BF16)
    g = proj(wgg)
    gg_o[...] = (g * _sigmoid(g)).astype(BF16)
    gr_o[...] = proj(wgr).astype(BF16)
    sq_o[...] = (proj(wsq) * (SB_HD ** -0.5 * LOG2E)).astype(BF16)
    sk_o[...] = proj(wsk).astype(BF16)
    sv_o[...] = proj(wsv).astype(BF16)
    g = proj(wsg)
    sg_o[...] = (g * _sigmoid(g)).astype(BF16)
    m_o[...] = _sigmoid(proj(wm) + bg_ref[...]).astype(BF16)


def _inproj(x2, norm_g, ws, b_gate, tm):
    m = x2.shape[0]
    widths = [w.shape[1] for w in ws]
    const = lambda i: (0, 0)
    w_specs = [pl.BlockSpec((D_MODEL, n), const, pipeline_mode=pl.Buffered(1)) for n in widths]
    in_specs = ([pl.BlockSpec((tm, D_MODEL), lambda i: (i, 0)),
                 pl.BlockSpec((1, D_MODEL), const)]
                + w_specs
                + [pl.BlockSpec((1, 2 * D_MODEL), const)])
    out_specs = [pl.BlockSpec((tm, n), lambda i: (i, 0)) for n in widths]
    out_shape = [jax.ShapeDtypeStruct((m, n), BF16) for n in widths]
    return pl.pallas_call(
        _inproj_kernel,
        grid=(m // tm,),
        in_specs=in_specs,
        out_specs=out_specs,
        out_shape=out_shape,
        compiler_params=pltpu.CompilerParams(
            dimension_semantics=("parallel",), vmem_limit_bytes=VMEM_LIMIT),
        name="inproj",
    )(x2, norm_g, *ws, b_gate)


def _gla_kernel(q_ref, k_ref, v_ref, gate_ref, r_ref, wdu_ref, bdec_ref, gng_ref, o_ref, st_ref,
                *, seq):
    c = GLA_CHUNK
    st_ref[...] = jnp.zeros_like(st_ref)
    row = lax.broadcasted_iota(jnp.int32, (c, c), 0)
    col = lax.broadcasted_iota(jnp.int32, (c, c), 1)
    tril = row >= col
    tril_b = jnp.where(tril, 1.0, 0.0).astype(BF16)

    def body(ci, carry):
        r0 = pl.multiple_of(ci * c, c)
        rows = pl.ds(r0, c)
        u = jnp.dot(r_ref[rows, :], wdu_ref[...], preferred_element_type=F32) + bdec_ref[...]
        la = -_softplus(-u) * (1.0 / GLA_TAU)
        la_hi = la.astype(BF16)
        la_lo = (la - la_hi.astype(F32)).astype(BF16)
        bcum = (jnp.dot(tril_b, la_hi, preferred_element_type=F32)
                + jnp.dot(tril_b, la_lo, preferred_element_type=F32))
        blast = bcum[c - 1:c, :]
        q = q_ref[rows, :].astype(F32)
        k = k_ref[rows, :].astype(F32)
        qe = (q * jnp.exp(bcum)).astype(BF16)
        ke = (k * jnp.exp(-bcum)).astype(BF16)
        kd = (k * jnp.exp(blast - bcum)).astype(BF16)
        dlast = jnp.exp(blast)
        for h in range(GLA_HEADS):
            ks = slice(h * GLA_HK, (h + 1) * GLA_HK)
            vs = slice(h * GLA_HV, (h + 1) * GLA_HV)
            v = v_ref[rows, vs]
            attn = lax.dot_general(qe[:, ks], ke[:, ks], _NT, preferred_element_type=F32)
            attn = jnp.where(tril, attn, 0.0).astype(BF16)
            st = st_ref[h]
            o = (jnp.dot(attn, v, preferred_element_type=F32)
                 + lax.dot_general(qe[:, ks], st.astype(BF16), _NT, preferred_element_type=F32))
            st_ref[h] = st * dlast[:, ks] + lax.dot_general(
                v, kd[:, ks], _TN, preferred_element_type=F32)
            on = o * lax.rsqrt(jnp.mean(o * o, axis=-1, keepdims=True) + EPS) * gng_ref[...]
            o_ref[rows, vs] = (on * gate_ref[rows, vs].astype(F32)).astype(BF16)
        return carry

    lax.fori_loop(0, seq // c, body, 0)


def _gla(gq, gk, gv, gg, gr, wdu, b_dec, gla_norm_g, batch, seq):
    const = lambda b: (0, 0)
    per_b = lambda n: pl.BlockSpec((seq, n), lambda b: (b, 0))
    return pl.pallas_call(
        functools.partial(_gla_kernel, seq=seq),
        grid=(batch,),
        in_specs=[per_b(GLA_DK), per_b(GLA_DK), per_b(GLA_DV), per_b(GLA_DV), per_b(RANK_PAD),
                  pl.BlockSpec((RANK_PAD, GLA_DK), const),
                  pl.BlockSpec((1, GLA_DK), const),
                  pl.BlockSpec((1, GLA_HV), const)],
        out_specs=per_b(GLA_DV),
        out_shape=jax.ShapeDtypeStruct((batch * seq, GLA_DV), BF16),
        scratch_shapes=[pltpu.VMEM((GLA_HEADS, GLA_HV, GLA_HK), F32)],
        compiler_params=pltpu.CompilerParams(
            dimension_semantics=("parallel",), vmem_limit_bytes=VMEM_LIMIT),
        name="gla",
    )(gq, gk, gv, gg, gr, wdu, b_dec, gla_norm_g)


def _sb_schedule(seq, t):
    i_tab, kb_tab = [], []
    for i in range(seq // t):
        for kb in range(i, -1, -1):
            i_tab.append(i)
            kb_tab.append(kb)
    n = len(i_tab)
    i_tab += [0] * (SB_STAGES - 1)
    kb_tab += [0] * (SB_STAGES - 1)
    return n, np.asarray(i_tab, np.int32), np.asarray(kb_tab, np.int32)


def _sb_kernel(itab_ref, kbtab_ref, q_ref, k_ref, v_ref, o_ref,
               acc_ref, sp_ref, lb_ref, a_ref, carry_ref, run_ref, bias_ref, tri_ref,
               *, t, nblk):
    row = lax.broadcasted_iota(jnp.int32, (t, t), 0)
    col = lax.broadcasted_iota(jnp.int32, (t, t), 1)
    tri_ref[...] = jnp.where(row > col, 1.0, 0.0).astype(BF16)
    bias_ref[0] = jnp.zeros((t, t), F32)
    bias_ref[1] = jnp.where(col < row, 0.0, MASK_BIAS)
    sp_ref[...] = jnp.zeros_like(sp_ref)
    lb_ref[...] = jnp.zeros_like(lb_ref)
    a_ref[...] = jnp.zeros_like(a_ref)
    carry_ref[...] = jnp.zeros_like(carry_ref)
    run_ref[...] = jnp.zeros_like(run_ref)
    acc_ref[pl.ds(0, t), :] = jnp.zeros((t, SB_WIDTH), F32)

    def trip(f, _):
        g3 = jnp.maximum(f - 2, 0)
        i3, kb3 = itab_ref[g3], kbtab_ref[g3]
        i1, kb1 = itab_ref[f], kbtab_ref[f]
        first3 = kb3 == i3
        first1 = kb1 == i1
        rows3 = pl.ds(pl.multiple_of(i3 * t, t), t)
        keys3 = pl.ds(pl.multiple_of(kb3 * t, t), t)
        rows1 = pl.ds(pl.multiple_of(i1 * t, t), t)
        keys1 = pl.ds(pl.multiple_of(kb1 * t, t), t)
        bias = bias_ref[first1.astype(jnp.int32)]
        for h in range(SB_HEADS):
            hs = slice(h * SB_HD, (h + 1) * SB_HD)
            pv = jnp.dot(a_ref[h], v_ref[keys3, hs], preferred_element_type=F32)
            acc_ref[rows3, hs] = jnp.where(first3, 0.0, acc_ref[rows3, hs]) + pv
            between = jnp.dot(sp_ref[h], tri_ref[...], preferred_element_type=F32)
            c = carry_ref[h]
            x = (lb_ref[h] - between) - jnp.concatenate([c, c], axis=1)
            a_ref[h] = jnp.exp2(x).astype(BF16)
            z = lax.dot_general(q_ref[rows1, hs], k_ref[keys1, hs], _NT,
                                preferred_element_type=F32) + bias
            neg_abs = pltpu.bitcast(pltpu.bitcast(z, jnp.uint32) | jnp.uint32(0x80000000), F32)
            sp = jnp.maximum(z, 0.0) + jnp.log(1.0 + jnp.exp2(neg_abs)) * LOG2E
            sp_ref[h] = sp.astype(BF16)
            lb_ref[h] = z - sp
            carry = jnp.where(first1, 0.0, run_ref[h])
            carry_ref[h] = carry
            run_ref[h] = carry + jnp.sum(sp, axis=1, keepdims=True)
        return 0

    lax.fori_loop(0, nblk + SB_STAGES - 1, trip, 0)

    def emit(r, _):
        rows = pl.ds(pl.multiple_of(r * t, t), t)
        o_ref[rows, :] = acc_ref[rows, :].astype(BF16)
        return 0

    lax.fori_loop(0, o_ref.shape[0] // t, emit, 0)


def _sb(sq, sk, sv, batch, seq, t):
    nblk, i_tab, kb_tab = _sb_schedule(seq, t)
    spec = pl.BlockSpec((seq, SB_WIDTH), lambda b, it, kt: (b, 0))
    grid_spec = pltpu.PrefetchScalarGridSpec(
        num_scalar_prefetch=2,
        grid=(batch,),
        in_specs=[spec, spec, spec],
        out_specs=spec,
        scratch_shapes=[
            pltpu.VMEM((seq, SB_WIDTH), F32),
            pltpu.VMEM((SB_HEADS, t, t), BF16),
            pltpu.VMEM((SB_HEADS, t, t), F32),
            pltpu.VMEM((SB_HEADS, t, t), BF16),
            pltpu.VMEM((SB_HEADS, t, LANES), F32),
            pltpu.VMEM((SB_HEADS, t, LANES), F32),
            pltpu.VMEM((2, t, t), F32),
            pltpu.VMEM((t, t), BF16),
        ])
    return pl.pallas_call(
        functools.partial(_sb_kernel, t=t, nblk=nblk),
        grid_spec=grid_spec,
        out_shape=jax.ShapeDtypeStruct((batch * seq, SB_WIDTH), BF16),
        compiler_params=pltpu.CompilerParams(
            dimension_semantics=("parallel",), vmem_limit_bytes=VMEM_LIMIT),
        name="stickbreak",
    )(jnp.asarray(i_tab), jnp.asarray(kb_tab), sq, sk, sv)


def _out_kernel(oa_ref, ob_ref, sg_ref, m_ref, x_ref, wpa_ref, wpb_ref, wo_ref, fg_ref, o_ref):
    ya = jnp.dot(oa_ref[...], wpa_ref[...], preferred_element_type=F32)
    yb = jnp.dot(ob_ref[...] * sg_ref[...], wpb_ref[...], preferred_element_type=F32)
    merged = (m_ref[:, :D_MODEL].astype(F32) * ya + m_ref[:, D_MODEL:].astype(F32) * yb)
    y = x_ref[...] + jnp.dot(merged.astype(BF16), wo_ref[...], preferred_element_type=F32)
    o_ref[...] = y * lax.rsqrt(jnp.mean(y * y, axis=-1, keepdims=True) + EPS) * fg_ref[...]


def _out(oa, ob, sg, gates, x2, wpa, wpb, wo, final_g, tm):
    m = x2.shape[0]
    const = lambda i: (0, 0)
    rows = lambda n: pl.BlockSpec((tm, n), lambda i: (i, 0))
    wspec = pl.BlockSpec((D_MODEL, D_MODEL), const, pipeline_mode=pl.Buffered(1))
    return pl.pallas_call(
        _out_kernel,
        grid=(m // tm,),
        in_specs=[rows(GLA_DV), rows(SB_WIDTH), rows(SB_WIDTH), rows(2 * D_MODEL), rows(D_MODEL),
                  wspec, wspec, wspec, pl.BlockSpec((1, D_MODEL), const)],
        out_specs=rows(D_MODEL),
        out_shape=jax.ShapeDtypeStruct((m, D_MODEL), F32),
        compiler_params=pltpu.CompilerParams(
            dimension_semantics=("parallel",), vmem_limit_bytes=VMEM_LIMIT),
        name="outproj",
    )(oa, ob, sg, gates, x2, wpa, wpb, wo, final_g)


def kernel(x, norm_g, w_in, w_dec_up, b_dec, gla_norm_g, w_pa, w_pb, b_gate, w_o, final_g):
    batch, seq, _ = x.shape
    x2 = x.reshape(batch * seq, D_MODEL)

    sizes = [GLA_DK, GLA_DK, GLA_DV, GLA_DV, GLA_RANK,
             SB_WIDTH, SB_WIDTH, SB_WIDTH, SB_WIDTH, 2 * D_MODEL]
    ws, off = [], 0
    for n in sizes:
        ws.append(w_in[:, off:off + n].astype(BF16))
        off += n
    ws[4] = jnp.pad(ws[4], ((0, 0), (0, RANK_PAD - GLA_RANK)))
    wdu = jnp.pad(w_dec_up, ((0, RANK_PAD - GLA_RANK), (0, 0))).astype(BF16)

    gq, gk, gv, gg, gr, sq, sk, sv, sg, gates = _inproj(
        x2, norm_g.reshape(1, D_MODEL), ws, b_gate.reshape(1, 2 * D_MODEL), tm=256)

    oa = _gla(gq, gk, gv, gg, gr, wdu, b_dec.reshape(1, GLA_DK),
              gla_norm_g.reshape(1, GLA_HV), batch, seq)
    ob = _sb(sq, sk, sv, batch, seq, t=256)

    out = _out(oa, ob, sg, gates, x2, w_pa.astype(BF16), w_pb.astype(BF16), w_o.astype(BF16),
               final_g.reshape(1, D_MODEL), tm=512)
    return out.reshape(batch, seq, D_MODEL)
```

```python
import functools

import numpy as np
import jax
import jax.numpy as jnp
from jax import lax
from jax.experimental import pallas as pl
from jax.experimental.pallas import tpu as pltpu

D_MODEL = 1024
GLA_HEADS = 4
GLA_HK = 128
GLA_HV = 256
GLA_DK = GLA_HEADS * GLA_HK
GLA_DV = GLA_HEADS * GLA_HV
GLA_RANK = 16
GLA_TAU = 16.0
GLA_CHUNK = 64
SB_HEADS = 8
SB_HD = 128
SB_WIDTH = SB_HEADS * SB_HD
EPS = 1e-6
LOG2E = 1.4426950408889634
SB_STAGES = 3
MASK_BIAS = -1e30

GLA_GROUP = 4
GLA_UNROLL = 4

LANES = 128
SUBLANES = 8
RANK_PAD = LANES
VMEM_LIMIT = 56 * 1024 * 1024

F32 = jnp.float32
BF16 = jnp.bfloat16

_NT = (((1,), (1,)), ((), ()))
_TN = (((0,), (0,)), ((), ()))


def _sigmoid(x):
    return 1.0 / (1.0 + jnp.exp(-x))


def _softplus(x):
    return jnp.maximum(x, 0.0) + jnp.log1p(jnp.exp(-jnp.abs(x)))


def _inproj_kernel(x_ref, ng_ref, wgq, wgk, wgv, wgg, wgr, wsq, wsk, wsv, wsg, wm, bg_ref,
                   gq_o, gk_o, gv_o, gg_o, gr_o, sq_o, sk_o, sv_o, sg_o, m_o):
    x = x_ref[...]
    h = x * lax.rsqrt(jnp.mean(x * x, axis=-1, keepdims=True) + EPS) * ng_ref[...]
    hb = h.astype(BF16)

    def proj(w_ref):
        return jnp.dot(hb, w_ref[...], preferred_element_type=F32)

    gq_o[...] = (proj(wgq) * (GLA_HK ** -0.5)).astype(BF16)
    gk_o[...] = proj(wgk).astype(BF16)
    gv_o[...] = proj(wgv).astype(BF16)
    g = proj(wgg)
    gg_o[...] = (g * _sigmoid(g)).astype(BF16)
    gr_o[...] = proj(wgr).astype(BF16)
    sq_o[...] = (proj(wsq) * (SB_HD ** -0.5 * LOG2E)).astype(BF16)
    sk_o[...] = proj(wsk).astype(BF16)
    sv_o[...] = proj(wsv).astype(BF16)
    g = proj(wsg)
    sg_o[...] = (g * _sigmoid(g)).astype(BF16)
    m_o[...] = _sigmoid(proj(wm) + bg_ref[...]).astype(BF16)


def _inproj(x2, norm_g, ws, b_gate, tm):
    m = x2.shape[0]
    widths = [w.shape[1] for w in ws]
    const = lambda i: (0, 0)
    w_specs = [pl.BlockSpec((D_MODEL, n), const, pipeline_mode=pl.Buffered(1)) for n in widths]
    in_specs = ([pl.BlockSpec((tm, D_MODEL), lambda i: (i, 0)),
                 pl.BlockSpec((1, D_MODEL), const)]
                + w_specs
                + [pl.BlockSpec((1, 2 * D_MODEL), const)])
    out_specs = [pl.BlockSpec((tm, n), lambda i: (i, 0)) for n in widths]
    out_shape = [jax.ShapeDtypeStruct((m, n), BF16) for n in widths]
    return pl.pallas_call(
        _inproj_kernel,
        grid=(m // tm,),
        in_specs=in_specs,
        out_specs=out_specs,
        out_shape=out_shape,
        compiler_params=pltpu.CompilerParams(
            dimension_semantics=("parallel",), vmem_limit_bytes=VMEM_LIMIT),
        name="inproj",
    )(x2, norm_g, *ws, b_gate)


def _gla_kernel(q_ref, k_ref, v_ref, gate_ref, r_ref, wdu_ref, bdec_ref, gng_ref, o_ref,
                st_ref, qe_ref, kd_ref, oi_ref, dl_ref, *, seq):
    c = GLA_CHUNK
    gr = GLA_GROUP * c
    row = lax.broadcasted_iota(jnp.int32, (gr, gr), 0)
    col = lax.broadcasted_iota(jnp.int32, (gr, gr), 1)
    bd_tril = (row >= col) & ((row ^ col) < c)
    bd_tril_b = jnp.where(bd_tril, 1.0, 0.0).astype(BF16)

    def phase1(gi, carry):
        rows = pl.ds(pl.multiple_of(gi * gr, gr), gr)
        u = jnp.dot(r_ref[rows, :], wdu_ref[...], preferred_element_type=F32) + bdec_ref[...]
        la = -_softplus(-u) * (1.0 / GLA_TAU)
        la_hi = la.astype(BF16)
        la_lo = (la - la_hi.astype(F32)).astype(BF16)
        bcum = (jnp.dot(bd_tril_b, la_hi, preferred_element_type=F32)
                + jnp.dot(bd_tril_b, la_lo, preferred_element_type=F32))
        lasts = [bcum[(j + 1) * c - 1:(j + 1) * c, :] for j in range(GLA_GROUP)]
        for j in range(GLA_GROUP):
            dl_ref[gi * GLA_GROUP + j] = jnp.broadcast_to(jnp.exp(lasts[j]), (SUBLANES, GLA_DK))
        blast = jnp.concatenate([jnp.broadcast_to(l, (c, GLA_DK)) for l in lasts], axis=0)
        q = q_ref[rows, :].astype(F32)
        k = k_ref[rows, :].astype(F32)
        qe = (q * jnp.exp(bcum)).astype(BF16)
        ke = (k * jnp.exp(-bcum)).astype(BF16)
        qe_ref[rows, :] = qe
        kd_ref[rows, :] = (k * jnp.exp(blast - bcum)).astype(BF16)
        for h in range(GLA_HEADS):
            ks = slice(h * GLA_HK, (h + 1) * GLA_HK)
            vs = slice(h * GLA_HV, (h + 1) * GLA_HV)
            attn = lax.dot_general(qe[:, ks], ke[:, ks], _NT, preferred_element_type=F32)
            attn = jnp.where(bd_tril, attn, 0.0).astype(BF16)
            oi_ref[rows, vs] = jnp.dot(attn, v_ref[rows, vs], preferred_element_type=F32)
        return carry

    lax.fori_loop(0, seq // gr, phase1, 0)

    st_ref[...] = jnp.zeros_like(st_ref)

    def phase2(ui, carry):
        sts = [st_ref[h] for h in range(GLA_HEADS)]
        for j in range(GLA_UNROLL):
            ci = ui * GLA_UNROLL + j
            rows = pl.ds(pl.multiple_of(ci * c, c), c)
            dlast = dl_ref[ci][0:1, :]
            for h in range(GLA_HEADS):
                ks = slice(h * GLA_HK, (h + 1) * GLA_HK)
                vs = slice(h * GLA_HV, (h + 1) * GLA_HV)
                o = oi_ref[rows, vs] + lax.dot_general(
                    qe_ref[rows, ks], sts[h].astype(BF16), _NT, preferred_element_type=F32)
                sts[h] = sts[h] * dlast[:, ks] + lax.dot_general(
                    v_ref[rows, vs], kd_ref[rows, ks], _TN, preferred_element_type=F32)
                on = o * lax.rsqrt(jnp.mean(o * o, axis=-1, keepdims=True) + EPS) * gng_ref[...]
                o_ref[rows, vs] = (on * gate_ref[rows, vs].astype(F32)).astype(BF16)
        for h in range(GLA_HEADS):
            st_ref[h] = sts[h]
        return carry

    lax.fori_loop(0, seq // (c * GLA_UNROLL), phase2, 0)


def _gla(gq, gk, gv, gg, gr, wdu, b_dec, gla_norm_g, batch, seq):
    const = lambda b: (0, 0)
    per_b = lambda n: pl.BlockSpec((seq, n), lambda b: (b, 0))
    return pl.pallas_call(
        functools.partial(_gla_kernel, seq=seq),
        grid=(batch,),
        in_specs=[per_b(GLA_DK), per_b(GLA_DK), per_b(GLA_DV), per_b(GLA_DV), per_b(RANK_PAD),
                  pl.BlockSpec((RANK_PAD, GLA_DK), const),
                  pl.BlockSpec((1, GLA_DK), const),
                  pl.BlockSpec((1, GLA_HV), const)],
        out_specs=per_b(GLA_DV),
        out_shape=jax.ShapeDtypeStruct((batch * seq, GLA_DV), BF16),
        scratch_shapes=[
            pltpu.VMEM((GLA_HEADS, GLA_HV, GLA_HK), F32),
            pltpu.VMEM((seq, GLA_DK), BF16),
            pltpu.VMEM((seq, GLA_DK), BF16),
            pltpu.VMEM((seq, GLA_DV), F32),
            pltpu.VMEM((seq // GLA_CHUNK, SUBLANES, GLA_DK), F32),
        ],
        compiler_params=pltpu.CompilerParams(
            dimension_semantics=("parallel",), vmem_limit_bytes=VMEM_LIMIT),
        name="gla",
    )(gq, gk, gv, gg, gr, wdu, b_dec, gla_norm_g)


def _sb_schedule(seq, t):
    i_tab, kb_tab = [], []
    for i in range(seq // t):
        for kb in range(i, -1, -1):
            i_tab.append(i)
            kb_tab.append(kb)
    n = len(i_tab)
    i_tab += [0] * (SB_STAGES - 1)
    kb_tab += [0] * (SB_STAGES - 1)
    return n, np.asarray(i_tab, np.int32), np.asarray(kb_tab, np.int32)


def _sb_kernel(itab_ref, kbtab_ref, q_ref, k_ref, v_ref, o_ref,
               acc_ref, sp_ref, lb_ref, a_ref, carry_ref, run_ref, bias_ref, tri_ref,
               *, t, nblk):
    row = lax.broadcasted_iota(jnp.int32, (t, t), 0)
    col = lax.broadcasted_iota(jnp.int32, (t, t), 1)
    tri_ref[...] = jnp.where(row > col, 1.0, 0.0).astype(BF16)
    bias_ref[0] = jnp.zeros((t, t), F32)
    bias_ref[1] = jnp.where(col < row, 0.0, MASK_BIAS)
    sp_ref[...] = jnp.zeros_like(sp_ref)
    lb_ref[...] = jnp.full(lb_ref.shape, MASK_BIAS, F32)
    a_ref[...] = jnp.zeros_like(a_ref)
    carry_ref[...] = jnp.zeros_like(carry_ref)
    run_ref[...] = jnp.zeros_like(run_ref)

    def clear(r, _):
        acc_ref[pl.ds(pl.multiple_of(r * t, t), t), :] = jnp.zeros((t, SB_WIDTH), F32)
        return 0

    lax.fori_loop(0, acc_ref.shape[0] // t, clear, 0)

    def trip(f, _):
        g3 = jnp.maximum(f - 2, 0)
        i3, kb3 = itab_ref[g3], kbtab_ref[g3]
        i1, kb1 = itab_ref[f], kbtab_ref[f]
        first1 = kb1 == i1
        rows3 = pl.ds(pl.multiple_of(i3 * t, t), t)
        keys3 = pl.ds(pl.multiple_of(kb3 * t, t), t)
        rows1 = pl.ds(pl.multiple_of(i1 * t, t), t)
        keys1 = pl.ds(pl.multiple_of(kb1 * t, t), t)
        bias = bias_ref[first1.astype(jnp.int32)]
        for h in range(SB_HEADS):
            hs = slice(h * SB_HD, (h + 1) * SB_HD)
            acc_ref[rows3, hs] += jnp.dot(a_ref[h], v_ref[keys3, hs],
                                          preferred_element_type=F32)
            between = jnp.dot(sp_ref[h], tri_ref[...], preferred_element_type=F32)
            c = carry_ref[h]
            x = (lb_ref[h] - between) - jnp.concatenate([c, c], axis=1)
            a_ref[h] = jnp.exp2(x).astype(BF16)
            z = lax.dot_general(q_ref[rows1, hs], k_ref[keys1, hs], _NT,
                                preferred_element_type=F32) + bias
            sp = jnp.maximum(z, 0.0) + jnp.log(1.0 + jnp.exp2(-jnp.abs(z))) * LOG2E
            sp_ref[h] = sp.astype(BF16)
            lb_ref[h] = z - sp
            carry = jnp.where(first1, 0.0, run_ref[h])
            carry_ref[h] = carry
            run_ref[h] = carry + jnp.sum(sp, axis=1, keepdims=True)
        return 0

    lax.fori_loop(0, nblk + SB_STAGES - 1, trip, 0)

    def emit(r, _):
        rows = pl.ds(pl.multiple_of(r * t, t), t)
        o_ref[rows, :] = acc_ref[rows, :].astype(BF16)
        return 0

    lax.fori_loop(0, o_ref.shape[0] // t, emit, 0)


def _sb(sq, sk, sv, batch, seq, t):
    nblk, i_tab, kb_tab = _sb_schedule(seq, t)
    spec = pl.BlockSpec((seq, SB_WIDTH), lambda b, it, kt: (b, 0))
    grid_spec = pltpu.PrefetchScalarGridSpec(
        num_scalar_prefetch=2,
        grid=(batch,),
        in_specs=[spec, spec, spec],
        out_specs=spec,
        scratch_shapes=[
            pltpu.VMEM((seq, SB_WIDTH), F32),
            pltpu.VMEM((SB_HEADS, t, t), BF16),
            pltpu.VMEM((SB_HEADS, t, t), F32),
            pltpu.VMEM((SB_HEADS, t, t), BF16),
            pltpu.VMEM((SB_HEADS, t, LANES), F32),
            pltpu.VMEM((SB_HEADS, t, LANES), F32),
            pltpu.VMEM((2, t, t), F32),
            pltpu.VMEM((t, t), BF16),
        ])
    return pl.pallas_call(
        functools.partial(_sb_kernel, t=t, nblk=nblk),
        grid_spec=grid_spec,
        out_shape=jax.ShapeDtypeStruct((batch * seq, SB_WIDTH), BF16),
        compiler_params=pltpu.CompilerParams(
            dimension_semantics=("parallel",), vmem_limit_bytes=VMEM_LIMIT),
        name="stickbreak",
    )(jnp.asarray(i_tab), jnp.asarray(kb_tab), sq, sk, sv)


def _out_kernel(oa_ref, ob_ref, sg_ref, m_ref, x_ref, wpa_ref, wpb_ref, wo_ref, fg_ref, o_ref):
    ya = jnp.dot(oa_ref[...], wpa_ref[...], preferred_element_type=F32)
    yb = jnp.dot(ob_ref[...] * sg_ref[...], wpb_ref[...], preferred_element_type=F32)
    merged = (m_ref[:, :D_MODEL].astype(F32) * ya + m_ref[:, D_MODEL:].astype(F32) * yb)
    y = x_ref[...] + jnp.dot(merged.astype(BF16), wo_ref[...], preferred_element_type=F32)
    o_ref[...] = y * lax.rsqrt(jnp.mean(y * y, axis=-1, keepdims=True) + EPS) * fg_ref[...]


def _out(oa, ob, sg, gates, x2, wpa, wpb, wo, final_g, tm):
    m = x2.shape[0]
    const = lambda i: (0, 0)
    rows = lambda n: pl.BlockSpec((tm, n), lambda i: (i, 0))
    wspec = pl.BlockSpec((D_MODEL, D_MODEL), const, pipeline_mode=pl.Buffered(1))
    return pl.pallas_call(
        _out_kernel,
        grid=(m // tm,),
        in_specs=[rows(GLA_DV), rows(SB_WIDTH), rows(SB_WIDTH), rows(2 * D_MODEL), rows(D_MODEL),
                  wspec, wspec, wspec, pl.BlockSpec((1, D_MODEL), const)],
        out_specs=rows(D_MODEL),
        out_shape=jax.ShapeDtypeStruct((m, D_MODEL), F32),
        compiler_params=pltpu.CompilerParams(
            dimension_semantics=("parallel",), vmem_limit_bytes=VMEM_LIMIT),
        name="outproj",
    )(oa, ob, sg, gates, x2, wpa, wpb, wo, final_g)


def kernel(x, norm_g, w_in, w_dec_up, b_dec, gla_norm_g, w_pa, w_pb, b_gate, w_o, final_g):
    batch, seq, _ = x.shape
    x2 = x.reshape(batch * seq, D_MODEL)

    sizes = [GLA_DK, GLA_DK, GLA_DV, GLA_DV, GLA_RANK,
             SB_WIDTH, SB_WIDTH, SB_WIDTH, SB_WIDTH, 2 * D_MODEL]
    ws, off = [], 0
    for n in sizes:
        ws.append(w_in[:, off:off + n].astype(BF16))
        off += n
    ws[4] = jnp.pad(ws[4], ((0, 0), (0, RANK_PAD - GLA_RANK)))
    wdu = jnp.pad(w_dec_up, ((0, RANK_PAD - GLA_RANK), (0, 0))).astype(BF16)

    gq, gk, gv, gg, gr, sq, sk, sv, sg, gates = _inproj(
        x2, norm_g.reshape(1, D_MODEL), ws, b_gate.reshape(1, 2 * D_MODEL), tm=256)

    oa = _gla(gq, gk, gv, gg, gr, wdu, b_dec.reshape(1, GLA_DK),
              gla_norm_g.reshape(1, GLA_HV), batch, seq)
    ob = _sb(sq, sk, sv, batch, seq, t=256)

    out = _out(oa, ob, sg, gates, x2, w_pa.astype(BF16), w_pb.astype(BF16), w_o.astype(BF16),
               final_g.reshape(1, D_MODEL), tm=512)
    return out.reshape(batch, seq, D_MODEL)
```

```python
import functools

import jax
import jax.numpy as jnp
from jax import lax
from jax.experimental import pallas as pl
from jax.experimental.pallas import tpu as pltpu

D_MODEL = 1024
GLA_HEADS = 4
GLA_HK = 128
GLA_HV = 256
GLA_DK = GLA_HEADS * GLA_HK
GLA_DV = GLA_HEADS * GLA_HV
GLA_RANK = 16
GLA_TAU = 16.0
GLA_CHUNK = 64
SB_HEADS = 8
SB_HD = 128
SB_WIDTH = SB_HEADS * SB_HD
EPS = 1e-6
LOG2E = 1.4426950408889634
SB_STAGES = 3
MASK_BIAS = -1e30
SB_ZERO_BITS = 160.0

GLA_GROUP = 4
GLA_UNROLL = 4

LANES = 128
SUBLANES = 8
RANK_PAD = LANES
VMEM_LIMIT = 56 * 1024 * 1024

F32 = jnp.float32
BF16 = jnp.bfloat16

_NT = (((1,), (1,)), ((), ()))
_TN = (((0,), (0,)), ((), ()))


def _sigmoid(x):
    return 1.0 / (1.0 + jnp.exp(-x))


def _softplus(x):
    return jnp.maximum(x, 0.0) + jnp.log1p(jnp.exp(-jnp.abs(x)))


def _inproj_kernel(x_ref, ng_ref, wgq, wgk, wgv, wgg, wgr, wsq, wsk, wsv, wsg, wm, bg_ref,
                   gq_o, gk_o, gv_o, gg_o, gr_o, sq_o, sk_o, sv_o, sg_o, m_o):
    x = x_ref[...]
    h = x * lax.rsqrt(jnp.mean(x * x, axis=-1, keepdims=True) + EPS) * ng_ref[...]
    hb = h.astype(BF16)

    def proj(w_ref):
        return jnp.dot(hb, w_ref[...], preferred_element_type=F32)

    gq_o[...] = (proj(wgq) * (GLA_HK ** -0.5)).astype(BF16)
    gk_o[...] = proj(wgk).astype(BF16)
    gv_o[...] = proj(wgv).astype(BF16)
    g = proj(wgg)
    gg_o[...] = (g * _sigmoid(g)).astype(BF16)
    gr_o[...] = proj(wgr).astype(BF16)
    sq_o[...] = (proj(wsq) * (SB_HD ** -0.5 * LOG2E)).astype(BF16)
    sk_o[...] = proj(wsk).astype(BF16)
    sv_o[...] = proj(wsv).astype(BF16)
    g = proj(wsg)
    sg_o[...] = (g * _sigmoid(g)).astype(BF16)
    m_o[...] = _sigmoid(proj(wm) + bg_ref[...]).astype(BF16)


def _inproj(x2, norm_g, ws, b_gate, tm):
    m = x2.shape[0]
    widths = [w.shape[1] for w in ws]
    const = lambda i: (0, 0)
    w_specs = [pl.BlockSpec((D_MODEL, n), const, pipeline_mode=pl.Buffered(1)) for n in widths]
    in_specs = ([pl.BlockSpec((tm, D_MODEL), lambda i: (i, 0)),
                 pl.BlockSpec((1, D_MODEL), const)]
                + w_specs
                + [pl.BlockSpec((1, 2 * D_MODEL), const)])
    out_specs = [pl.BlockSpec((tm, n), lambda i: (i, 0)) for n in widths]
    out_shape = [jax.ShapeDtypeStruct((m, n), BF16) for n in widths]
    return pl.pallas_call(
        _inproj_kernel,
        grid=(m // tm,),
        in_specs=in_specs,
        out_specs=out_specs,
        out_shape=out_shape,
        compiler_params=pltpu.CompilerParams(
            dimension_semantics=("parallel",), vmem_limit_bytes=VMEM_LIMIT),
        name="inproj",
    )(x2, norm_g, *ws, b_gate)


def _gla_kernel(q_ref, k_ref, v_ref, gate_ref, r_ref, wdu_ref, bdec_ref, gng_ref, o_ref,
                st_ref, qe_ref, kd_ref, oi_ref, dl_ref, *, seq):
    c = GLA_CHUNK
    gr = GLA_GROUP * c
    row = lax.broadcasted_iota(jnp.int32, (gr, gr), 0)
    col = lax.broadcasted_iota(jnp.int32, (gr, gr), 1)
    bd_tril = (row >= col) & ((row ^ col) < c)
    bd_tril_b = jnp.where(bd_tril, 1.0, 0.0).astype(BF16)

    def phase1(gi, carry):
        rows = pl.ds(pl.multiple_of(gi * gr, gr), gr)
        u = jnp.dot(r_ref[rows, :], wdu_ref[...], preferred_element_type=F32) + bdec_ref[...]
        la = -_softplus(-u) * (1.0 / GLA_TAU)
        la_hi = la.astype(BF16)
        la_lo = (la - la_hi.astype(F32)).astype(BF16)
        bcum = (jnp.dot(bd_tril_b, la_hi, preferred_element_type=F32)
                + jnp.dot(bd_tril_b, la_lo, preferred_element_type=F32))
        lasts = [bcum[(j + 1) * c - 1:(j + 1) * c, :] for j in range(GLA_GROUP)]
        for j in range(GLA_GROUP):
            dl_ref[gi * GLA_GROUP + j] = jnp.broadcast_to(jnp.exp(lasts[j]), (SUBLANES, GLA_DK))
        blast = jnp.concatenate([jnp.broadcast_to(l, (c, GLA_DK)) for l in lasts], axis=0)
        q = q_ref[rows, :].astype(F32)
        k = k_ref[rows, :].astype(F32)
        qe = (q * jnp.exp(bcum)).astype(BF16)
        ke = (k * jnp.exp(-bcum)).astype(BF16)
        qe_ref[rows, :] = qe
        kd_ref[rows, :] = (k * jnp.exp(blast - bcum)).astype(BF16)
        for h in range(GLA_HEADS):
            ks = slice(h * GLA_HK, (h + 1) * GLA_HK)
            vs = slice(h * GLA_HV, (h + 1) * GLA_HV)
            attn = lax.dot_general(qe[:, ks], ke[:, ks], _NT, preferred_element_type=F32)
            attn = jnp.where(bd_tril, attn, 0.0).astype(BF16)
            oi_ref[rows, vs] = jnp.dot(attn, v_ref[rows, vs], preferred_element_type=F32)
        return carry

    lax.fori_loop(0, seq // gr, phase1, 0)

    st_ref[...] = jnp.zeros_like(st_ref)

    def phase2(ui, carry):
        sts = [st_ref[h] for h in range(GLA_HEADS)]
        for j in range(GLA_UNROLL):
            ci = ui * GLA_UNROLL + j
            rows = pl.ds(pl.multiple_of(ci * c, c), c)
            dlast = dl_ref[ci][0:1, :]
            for h in range(GLA_HEADS):
                ks = slice(h * GLA_HK, (h + 1) * GLA_HK)
                vs = slice(h * GLA_HV, (h + 1) * GLA_HV)
                o = oi_ref[rows, vs] + lax.dot_general(
                    qe_ref[rows, ks], sts[h].astype(BF16), _NT, preferred_element_type=F32)
                sts[h] = sts[h] * dlast[:, ks] + lax.dot_general(
                    v_ref[rows, vs], kd_ref[rows, ks], _TN, preferred_element_type=F32)
                on = o * lax.rsqrt(jnp.mean(o * o, axis=-1, keepdims=True) + EPS) * gng_ref[...]
                o_ref[rows, vs] = (on * gate_ref[rows, vs].astype(F32)).astype(BF16)
        for h in range(GLA_HEADS):
            st_ref[h] = sts[h]
        return carry

    lax.fori_loop(0, seq // (c * GLA_UNROLL), phase2, 0)


def _gla(gq, gk, gv, gg, gr, wdu, b_dec, gla_norm_g, batch, seq):
    const = lambda b: (0, 0)
    per_b = lambda n: pl.BlockSpec((seq, n), lambda b: (b, 0))
    return pl.pallas_call(
        functools.partial(_gla_kernel, seq=seq),
        grid=(batch,),
        in_specs=[per_b(GLA_DK), per_b(GLA_DK), per_b(GLA_DV), per_b(GLA_DV), per_b(RANK_PAD),
                  pl.BlockSpec((RANK_PAD, GLA_DK), const),
                  pl.BlockSpec((1, GLA_DK), const),
                  pl.BlockSpec((1, GLA_HV), const)],
        out_specs=per_b(GLA_DV),
        out_shape=jax.ShapeDtypeStruct((batch * seq, GLA_DV), BF16),
        scratch_shapes=[
            pltpu.VMEM((GLA_HEADS, GLA_HV, GLA_HK), F32),
            pltpu.VMEM((seq, GLA_DK), BF16),
            pltpu.VMEM((seq, GLA_DK), BF16),
            pltpu.VMEM((seq, GLA_DV), F32),
            pltpu.VMEM((seq // GLA_CHUNK, SUBLANES, GLA_DK), F32),
        ],
        compiler_params=pltpu.CompilerParams(
            dimension_semantics=("parallel",), vmem_limit_bytes=VMEM_LIMIT),
        name="gla",
    )(gq, gk, gv, gg, gr, wdu, b_dec, gla_norm_g)


def _sb_kernel(q_ref, k_ref, v_ref, o_ref,
               acc_ref, sp_ref, lb_ref, a_ref, carry_ref, run_ref, bias_ref, tri_ref, *, t):
    nq = q_ref.shape[0] // t
    row = lax.broadcasted_iota(jnp.int32, (t, t), 0)
    col = lax.broadcasted_iota(jnp.int32, (t, t), 1)
    tri_ref[...] = jnp.where(row > col, 1.0, 0.0).astype(BF16)
    bias_ref[0] = jnp.zeros((t, t), F32)
    bias_ref[1] = jnp.where(col < row, 0.0, MASK_BIAS)
    bias_ref[2] = jnp.full((t, t), MASK_BIAS, F32)
    sp_ref[...] = jnp.zeros_like(sp_ref)
    lb_ref[...] = jnp.full(lb_ref.shape, MASK_BIAS, F32)
    a_ref[...] = jnp.zeros_like(a_ref)
    carry_ref[...] = jnp.zeros_like(carry_ref)
    run_ref[...] = jnp.zeros_like(run_ref)

    def clear(r, _):
        acc_ref[pl.ds(pl.multiple_of(r * t, t), t), :] = jnp.zeros((t, SB_WIDTH), F32)
        return 0

    lax.fori_loop(0, nq, clear, 0)

    def trip(state):
        i1, kb1, i2, kb2, i3, kb3 = state
        live = i1 < nq
        ic = jnp.minimum(i1, nq - 1)
        kc = jnp.minimum(kb1, ic)
        first1 = kb1 == i1
        rows3 = pl.ds(pl.multiple_of(i3 * t, t), t)
        keys3 = pl.ds(pl.multiple_of(kb3 * t, t), t)
        rows1 = pl.ds(pl.multiple_of(ic * t, t), t)
        keys1 = pl.ds(pl.multiple_of(kc * t, t), t)
        bias = bias_ref[jnp.where(live, first1.astype(jnp.int32), 2)]
        run_min = None
        for h in range(SB_HEADS):
            hs = slice(h * SB_HD, (h + 1) * SB_HD)
            acc_ref[rows3, hs] += jnp.dot(a_ref[h], v_ref[keys3, hs],
                                          preferred_element_type=F32)
            between = jnp.dot(sp_ref[h], tri_ref[...], preferred_element_type=F32)
            c = carry_ref[h]
            x = (lb_ref[h] - between) - jnp.concatenate([c, c], axis=1)
            a_ref[h] = jnp.exp2(x).astype(BF16)
            z = lax.dot_general(q_ref[rows1, hs], k_ref[keys1, hs], _NT,
                                preferred_element_type=F32) + bias
            sp = jnp.maximum(z, 0.0) + jnp.log(1.0 + jnp.exp2(-jnp.abs(z))) * LOG2E
            sp_ref[h] = sp.astype(BF16)
            lb_ref[h] = z - sp
            carry = jnp.where(first1, 0.0, run_ref[h])
            carry_ref[h] = carry
            run = carry + jnp.sum(sp, axis=1, keepdims=True)
            run_ref[h] = run
            run_min = run if run_min is None else jnp.minimum(run_min, run)
        walk_on = live & (kb1 > 0) & (jnp.min(run_min) <= SB_ZERO_BITS)
        i_next = jnp.where(walk_on, i1, i1 + 1)
        kb_next = jnp.where(walk_on, kb1 - 1, i1 + 1)
        return i_next, kb_next, ic, kc, i2, kb2

    zero = jnp.int32(0)
    lax.while_loop(lambda s: s[0] < nq + (SB_STAGES - 1), trip,
                   (zero, zero, zero, zero, zero, zero))

    def emit(r, _):
        rows = pl.ds(pl.multiple_of(r * t, t), t)
        o_ref[rows, :] = acc_ref[rows, :].astype(BF16)
        return 0

    lax.fori_loop(0, nq, emit, 0)


def _sb(sq, sk, sv, batch, seq, t):
    spec = pl.BlockSpec((seq, SB_WIDTH), lambda b: (b, 0))
    return pl.pallas_call(
        functools.partial(_sb_kernel, t=t),
        grid=(batch,),
        in_specs=[spec, spec, spec],
        out_specs=spec,
        out_shape=jax.ShapeDtypeStruct((batch * seq, SB_WIDTH), BF16),
        scratch_shapes=[
            pltpu.VMEM((seq, SB_WIDTH), F32),
            pltpu.VMEM((SB_HEADS, t, t), BF16),
            pltpu.VMEM((SB_HEADS, t, t), F32),
            pltpu.VMEM((SB_HEADS, t, t), BF16),
            pltpu.VMEM((SB_HEADS, t, LANES), F32),
            pltpu.VMEM((SB_HEADS, t, LANES), F32),
            pltpu.VMEM((3, t, t), F32),
            pltpu.VMEM((t, t), BF16),
        ],
        compiler_params=pltpu.CompilerParams(
            dimension_semantics=("parallel",), vmem_limit_bytes=VMEM_LIMIT),
        name="stickbreak",
    )(sq, sk, sv)


def _out_kernel(oa_ref, ob_ref, sg_ref, m_ref, x_ref, wpa_ref, wpb_ref, wo_ref, fg_ref, o_ref):
    ya = jnp.dot(oa_ref[...], wpa_ref[...], preferred_element_type=F32)
    yb = jnp.dot(ob_ref[...] * sg_ref[...], wpb_ref[...], preferred_element_type=F32)
    merged = (m_ref[:, :D_MODEL].astype(F32) * ya + m_ref[:, D_MODEL:].astype(F32) * yb)
    y = x_ref[...] + jnp.dot(merged.astype(BF16), wo_ref[...], preferred_element_type=F32)
    o_ref[...] = y * lax.rsqrt(jnp.mean(y * y, axis=-1, keepdims=True) + EPS) * fg_ref[...]


def _out(oa, ob, sg, gates, x2, wpa, wpb, wo, final_g, tm):
    m = x2.shape[0]
    const = lambda i: (0, 0)
    rows = lambda n: pl.BlockSpec((tm, n), lambda i: (i, 0))
    wspec = pl.BlockSpec((D_MODEL, D_MODEL), const, pipeline_mode=pl.Buffered(1))
    return pl.pallas_call(
        _out_kernel,
        grid=(m // tm,),
        in_specs=[rows(GLA_DV), rows(SB_WIDTH), rows(SB_WIDTH), rows(2 * D_MODEL), rows(D_MODEL),
                  wspec, wspec, wspec, pl.BlockSpec((1, D_MODEL), const)],
        out_specs=rows(D_MODEL),
        out_shape=jax.ShapeDtypeStruct((m, D_MODEL), F32),
        compiler_params=pltpu.CompilerParams(
            dimension_semantics=("parallel",), vmem_limit_bytes=VMEM_LIMIT),
        name="outproj",
    )(oa, ob, sg, gates, x2, wpa, wpb, wo, final_g)


def kernel(x, norm_g, w_in, w_dec_up, b_dec, gla_norm_g, w_pa, w_pb, b_gate, w_o, final_g):
    batch, seq, _ = x.shape
    x2 = x.reshape(batch * seq, D_MODEL)

    sizes = [GLA_DK, GLA_DK, GLA_DV, GLA_DV, GLA_RANK,
             SB_WIDTH, SB_WIDTH, SB_WIDTH, SB_WIDTH, 2 * D_MODEL]
    ws, off = [], 0
    for n in sizes:
        ws.append(w_in[:, off:off + n].astype(BF16))
        off += n
    ws[4] = jnp.pad(ws[4], ((0, 0), (0, RANK_PAD - GLA_RANK)))
    wdu = jnp.pad(w_dec_up, ((0, RANK_PAD - GLA_RANK), (0, 0))).astype(BF16)

    gq, gk, gv, gg, gr, sq, sk, sv, sg, gates = _inproj(
        x2, norm_g.reshape(1, D_MODEL), ws, b_gate.reshape(1, 2 * D_MODEL), tm=256)

    oa = _gla(gq, gk, gv, gg, gr, wdu, b_dec.reshape(1, GLA_DK),
              gla_norm_g.reshape(1, GLA_HV), batch, seq)
    ob = _sb(sq, sk, sv, batch, seq, t=256)

    out = _out(oa, ob, sg, gates, x2, w_pa.astype(BF16), w_pb.astype(BF16), w_o.astype(BF16),
               final_g.reshape(1, D_MODEL), tm=512)
    return out.reshape(batch, seq, D_MODEL)
```

```python
import functools

import jax
import jax.numpy as jnp
from jax import lax
from jax.experimental import pallas as pl
from jax.experimental.pallas import tpu as pltpu

D_MODEL = 1024
GLA_HEADS = 4
GLA_HK = 128
GLA_HV = 256
GLA_DK = GLA_HEADS * GLA_HK
GLA_DV = GLA_HEADS * GLA_HV
GLA_RANK = 16
GLA_TAU = 16.0
GLA_CHUNK = 64
SB_HEADS = 8
SB_HD = 128
SB_WIDTH = SB_HEADS * SB_HD
EPS = 1e-6
LOG2E = 1.4426950408889634
SB_STAGES = 3
MASK_BIAS = -1e30
SB_ZERO_BITS = 160.0

GLA_GROUP = 4
GLA_UNROLL = 4

LANES = 128
SUBLANES = 8
RANK_PAD = LANES
VMEM_LIMIT = 56 * 1024 * 1024

F32 = jnp.float32
BF16 = jnp.bfloat16

_NT = (((1,), (1,)), ((), ()))
_TN = (((0,), (0,)), ((), ()))


def _sigmoid(x):
    return 1.0 / (1.0 + jnp.exp(-x))


def _softplus(x):
    return jnp.maximum(x, 0.0) + jnp.log(1.0 + jnp.exp(-jnp.abs(x)))


def _inproj_kernel(x_ref, ng_ref, wgq, wgk, wgv, wgg, wgr, wsq, wsk, wsv, wsg, wm, bg_ref,
                   gq_o, gk_o, gv_o, gg_o, gr_o, sq_o, sk_o, sv_o, sg_o, m_o):
    x = x_ref[...]
    h = x * lax.rsqrt(jnp.mean(x * x, axis=-1, keepdims=True) + EPS) * ng_ref[...]
    hb = h.astype(BF16)

    def proj(w_ref):
        return jnp.dot(hb, w_ref[...], preferred_element_type=F32)

    gq_o[...] = (proj(wgq) * (GLA_HK ** -0.5)).astype(BF16)
    gk_o[...] = proj(wgk).astype(BF16)
    gv_o[...] = proj(wgv).astype(BF16)
    g = proj(wgg)
    gg_o[...] = (g * _sigmoid(g)).astype(BF16)
    gr_o[...] = proj(wgr).astype(BF16)
    sq_o[...] = (proj(wsq) * (SB_HD ** -0.5 * LOG2E)).astype(BF16)
    sk_o[...] = proj(wsk).astype(BF16)
    sv_o[...] = proj(wsv).astype(BF16)
    g = proj(wsg)
    sg_o[...] = (g * _sigmoid(g)).astype(BF16)
    m_o[...] = _sigmoid(proj(wm) + bg_ref[...]).astype(BF16)


_W_HEAD, _W_RANK, _W_TAIL = 0, 1, 2
_W_BLOCKS = ((_W_HEAD, GLA_DK, 0), (_W_HEAD, GLA_DK, 1), (_W_HEAD, GLA_DV, 1), (_W_HEAD, GLA_DV, 2),
             (_W_RANK, RANK_PAD, 0),
             (_W_TAIL, SB_WIDTH, 0), (_W_TAIL, SB_WIDTH, 1), (_W_TAIL, SB_WIDTH, 2),
             (_W_TAIL, SB_WIDTH, 3), (_W_TAIL, 2 * D_MODEL, 2))


def _split_w_in(w_in):
    head = 2 * GLA_DK + 2 * GLA_DV
    w_head = w_in[:, :head].astype(BF16)
    w_rank = jnp.pad(w_in[:, head:head + GLA_RANK], ((0, 0), (0, RANK_PAD - GLA_RANK))).astype(BF16)
    w_tail = w_in[:, head + GLA_RANK:].astype(BF16)
    return w_head, w_rank, w_tail


def _inproj(x2, norm_g, w_parts, b_gate, tm):
    m = x2.shape[0]
    widths = [n for _, n, _ in _W_BLOCKS]
    const = lambda i: (0, 0)
    w_specs = [pl.BlockSpec((D_MODEL, n), functools.partial(lambda i, c: (0, c), c=c),
                            pipeline_mode=pl.Buffered(1))
               for _, n, c in _W_BLOCKS]
    in_specs = ([pl.BlockSpec((tm, D_MODEL), lambda i: (i, 0)),
                 pl.BlockSpec((1, D_MODEL), const)]
                + w_specs
                + [pl.BlockSpec((1, 2 * D_MODEL), const)])
    out_specs = [pl.BlockSpec((tm, n), lambda i: (i, 0)) for n in widths]
    out_shape = [jax.ShapeDtypeStruct((m, n), BF16) for n in widths]
    return pl.pallas_call(
        _inproj_kernel,
        grid=(m // tm,),
        in_specs=in_specs,
        out_specs=out_specs,
        out_shape=out_shape,
        compiler_params=pltpu.CompilerParams(
            dimension_semantics=("parallel",), vmem_limit_bytes=VMEM_LIMIT),
        name="inproj",
    )(x2, norm_g, *[w_parts[src] for src, _, _ in _W_BLOCKS], b_gate)


def _gla_kernel(q_ref, k_ref, v_ref, gate_ref, r_ref, wdu_ref, bdec_ref, gng_ref, o_ref,
                st_ref, qe_ref, kd_ref, oi_ref, dl_ref, *, seq):
    c = GLA_CHUNK
    gr = GLA_GROUP * c
    row = lax.broadcasted_iota(jnp.int32, (gr, gr), 0)
    col = lax.broadcasted_iota(jnp.int32, (gr, gr), 1)
    bd_tril = (row >= col) & ((row ^ col) < c)
    bd_tril_b = jnp.where(bd_tril, 1.0, 0.0).astype(BF16)

    def phase1(gi, carry):
        rows = pl.ds(pl.multiple_of(gi * gr, gr), gr)
        u = jnp.dot(r_ref[rows, :], wdu_ref[...], preferred_element_type=F32) + bdec_ref[...]
        la = -_softplus(-u) * (1.0 / GLA_TAU)
        la_hi = la.astype(BF16)
        la_lo = (la - la_hi.astype(F32)).astype(BF16)
        bcum = (jnp.dot(bd_tril_b, la_hi, preferred_element_type=F32)
                + jnp.dot(bd_tril_b, la_lo, preferred_element_type=F32))
        lasts = [bcum[(j + 1) * c - 1:(j + 1) * c, :] for j in range(GLA_GROUP)]
        for j in range(GLA_GROUP):
            dl_ref[gi * GLA_GROUP + j] = jnp.broadcast_to(jnp.exp(lasts[j]), (SUBLANES, GLA_DK))
        blast = jnp.concatenate([jnp.broadcast_to(l, (c, GLA_DK)) for l in lasts], axis=0)
        q = q_ref[rows, :].astype(F32)
        k = k_ref[rows, :].astype(F32)
        qe = (q * jnp.exp(bcum)).astype(BF16)
        ke = (k * jnp.exp(-bcum)).astype(BF16)
        qe_ref[rows, :] = qe
        kd_ref[rows, :] = (k * jnp.exp(blast - bcum)).astype(BF16)
        for h in range(GLA_HEADS):
            ks = slice(h * GLA_HK, (h + 1) * GLA_HK)
            vs = slice(h * GLA_HV, (h + 1) * GLA_HV)
            attn = lax.dot_general(qe[:, ks], ke[:, ks], _NT, preferred_element_type=F32)
            attn = jnp.where(bd_tril, attn, 0.0).astype(BF16)
            oi_ref[rows, vs] = jnp.dot(attn, v_ref[rows, vs], preferred_element_type=F32)
        return carry

    lax.fori_loop(0, seq // gr, phase1, 0)

    st_ref[...] = jnp.zeros_like(st_ref)

    def phase2(ui, carry):
        sts = [st_ref[h] for h in range(GLA_HEADS)]
        for j in range(GLA_UNROLL):
            ci = ui * GLA_UNROLL + j
            rows = pl.ds(pl.multiple_of(ci * c, c), c)
            dlast = dl_ref[ci][0:1, :]
            for h in range(GLA_HEADS):
                ks = slice(h * GLA_HK, (h + 1) * GLA_HK)
                vs = slice(h * GLA_HV, (h + 1) * GLA_HV)
                o = oi_ref[rows, vs] + lax.dot_general(
                    qe_ref[rows, ks], sts[h].astype(BF16), _NT, preferred_element_type=F32)
                sts[h] = sts[h] * dlast[:, ks] + lax.dot_general(
                    v_ref[rows, vs], kd_ref[rows, ks], _TN, preferred_element_type=F32)
                on = o * lax.rsqrt(jnp.mean(o * o, axis=-1, keepdims=True) + EPS) * gng_ref[...]
                o_ref[rows, vs] = (on * gate_ref[rows, vs].astype(F32)).astype(BF16)
        for h in range(GLA_HEADS):
            st_ref[h] = sts[h]
        return carry

    lax.fori_loop(0, seq // (c * GLA_UNROLL), phase2, 0)


def _gla(gq, gk, gv, gg, gr, wdu, b_dec, gla_norm_g, batch, seq):
    const = lambda b: (0, 0)
    per_b = lambda n: pl.BlockSpec((seq, n), lambda b: (b, 0))
    return pl.pallas_call(
        functools.partial(_gla_kernel, seq=seq),
        grid=(batch,),
        in_specs=[per_b(GLA_DK), per_b(GLA_DK), per_b(GLA_DV), per_b(GLA_DV), per_b(RANK_PAD),
                  pl.BlockSpec((RANK_PAD, GLA_DK), const),
                  pl.BlockSpec((1, GLA_DK), const),
                  pl.BlockSpec((1, GLA_HV), const)],
        out_specs=per_b(GLA_DV),
        out_shape=jax.ShapeDtypeStruct((batch * seq, GLA_DV), BF16),
        scratch_shapes=[
            pltpu.VMEM((GLA_HEADS, GLA_HV, GLA_HK), F32),
            pltpu.VMEM((seq, GLA_DK), BF16),
            pltpu.VMEM((seq, GLA_DK), BF16),
            pltpu.VMEM((seq, GLA_DV), F32),
            pltpu.VMEM((seq // GLA_CHUNK, SUBLANES, GLA_DK), F32),
        ],
        compiler_params=pltpu.CompilerParams(
            dimension_semantics=("parallel",), vmem_limit_bytes=VMEM_LIMIT),
        name="gla",
    )(gq, gk, gv, gg, gr, wdu, b_dec, gla_norm_g)


def _sb_kernel(q_ref, k_ref, v_ref, o_ref,
               acc_ref, sp_ref, lb_ref, a_ref, carry_ref, run_ref, bias_ref, tri_ref, *, t):
    nq = q_ref.shape[0] // t
    row = lax.broadcasted_iota(jnp.int32, (t, t), 0)
    col = lax.broadcasted_iota(jnp.int32, (t, t), 1)
    tri_ref[...] = jnp.where(row > col, 1.0, 0.0).astype(BF16)
    bias_ref[0] = jnp.zeros((t, t), F32)
    bias_ref[1] = jnp.where(col < row, 0.0, MASK_BIAS)
    bias_ref[2] = jnp.full((t, t), MASK_BIAS, F32)
    sp_ref[...] = jnp.zeros_like(sp_ref)
    lb_ref[...] = jnp.full(lb_ref.shape, MASK_BIAS, F32)
    a_ref[...] = jnp.zeros_like(a_ref)
    carry_ref[...] = jnp.zeros_like(carry_ref)
    run_ref[...] = jnp.zeros_like(run_ref)

    def clear(r, _):
        acc_ref[pl.ds(pl.multiple_of(r * t, t), t), :] = jnp.zeros((t, SB_WIDTH), F32)
        return 0

    lax.fori_loop(0, nq, clear, 0)

    def trip(state):
        i1, kb1, i2, kb2, i3, kb3 = state
        live = i1 < nq
        ic = jnp.minimum(i1, nq - 1)
        kc = jnp.minimum(kb1, ic)
        first1 = kb1 == i1
        rows3 = pl.ds(pl.multiple_of(i3 * t, t), t)
        keys3 = pl.ds(pl.multiple_of(kb3 * t, t), t)
        rows1 = pl.ds(pl.multiple_of(ic * t, t), t)
        keys1 = pl.ds(pl.multiple_of(kc * t, t), t)
        bias = bias_ref[jnp.where(live, first1.astype(jnp.int32), 2)]
        run_min = None
        for h in range(SB_HEADS):
            hs = slice(h * SB_HD, (h + 1) * SB_HD)
            acc_ref[rows3, hs] += jnp.dot(a_ref[h], v_ref[keys3, hs],
                                          preferred_element_type=F32)
            between = jnp.dot(sp_ref[h], tri_ref[...], preferred_element_type=F32)
            c = carry_ref[h]
            x = (lb_ref[h] - between) - jnp.concatenate([c, c], axis=1)
            a_ref[h] = jnp.exp2(x).astype(BF16)
            z = lax.dot_general(q_ref[rows1, hs], k_ref[keys1, hs], _NT,
                                preferred_element_type=F32) + bias
            sp = jnp.maximum(z, 0.0) + jnp.log(1.0 + jnp.exp2(-jnp.abs(z))) * LOG2E
            sp_ref[h] = sp.astype(BF16)
            lb_ref[h] = z - sp
            carry = jnp.where(first1, 0.0, run_ref[h])
            carry_ref[h] = carry
            run = carry + jnp.sum(sp, axis=1, keepdims=True)
            run_ref[h] = run
            run_min = run if run_min is None else jnp.minimum(run_min, run)
        walk_on = live & (kb1 > 0) & (jnp.min(run_min) <= SB_ZERO_BITS)
        i_next = jnp.where(walk_on, i1, i1 + 1)
        kb_next = jnp.where(walk_on, kb1 - 1, i1 + 1)
        return i_next, kb_next, ic, kc, i2, kb2

    zero = jnp.int32(0)
    lax.while_loop(lambda s: s[0] < nq + (SB_STAGES - 1), trip,
                   (zero, zero, zero, zero, zero, zero))

    def emit(r, _):
        rows = pl.ds(pl.multiple_of(r * t, t), t)
        o_ref[rows, :] = acc_ref[rows, :].astype(BF16)
        return 0

    lax.fori_loop(0, nq, emit, 0)


def _sb(sq, sk, sv, batch, seq, t):
    spec = pl.BlockSpec((seq, SB_WIDTH), lambda b: (b, 0))
    return pl.pallas_call(
        functools.partial(_sb_kernel, t=t),
        grid=(batch,),
        in_specs=[spec, spec, spec],
        out_specs=spec,
        out_shape=jax.ShapeDtypeStruct((batch * seq, SB_WIDTH), BF16),
        scratch_shapes=[
            pltpu.VMEM((seq, SB_WIDTH), F32),
            pltpu.VMEM((SB_HEADS, t, t), BF16),
            pltpu.VMEM((SB_HEADS, t, t), F32),
            pltpu.VMEM((SB_HEADS, t, t), BF16),
            pltpu.VMEM((SB_HEADS, t, LANES), F32),
            pltpu.VMEM((SB_HEADS, t, LANES), F32),
            pltpu.VMEM((3, t, t), F32),
            pltpu.VMEM((t, t), BF16),
        ],
        compiler_params=pltpu.CompilerParams(
            dimension_semantics=("parallel",), vmem_limit_bytes=VMEM_LIMIT),
        name="stickbreak",
    )(sq, sk, sv)


def _out_kernel(oa_ref, ob_ref, sg_ref, m_ref, x_ref, wpa_ref, wpb_ref, wo_ref, fg_ref, o_ref):
    ya = jnp.dot(oa_ref[...], wpa_ref[...], preferred_element_type=F32)
    yb = jnp.dot(ob_ref[...] * sg_ref[...], wpb_ref[...], preferred_element_type=F32)
    merged = (m_ref[:, :D_MODEL].astype(F32) * ya + m_ref[:, D_MODEL:].astype(F32) * yb)
    y = x_ref[...] + jnp.dot(merged.astype(BF16), wo_ref[...], preferred_element_type=F32)
    o_ref[...] = y * lax.rsqrt(jnp.mean(y * y, axis=-1, keepdims=True) + EPS) * fg_ref[...]


def _out(oa, ob, sg, gates, x2, wpa, wpb, wo, final_g, tm):
    m = x2.shape[0]
    const = lambda i: (0, 0)
    rows = lambda n: pl.BlockSpec((tm, n), lambda i: (i, 0))
    wspec = pl.BlockSpec((D_MODEL, D_MODEL), const, pipeline_mode=pl.Buffered(1))
    return pl.pallas_call(
        _out_kernel,
        grid=(m // tm,),
        in_specs=[rows(GLA_DV), rows(SB_WIDTH), rows(SB_WIDTH), rows(2 * D_MODEL), rows(D_MODEL),
                  wspec, wspec, wspec, pl.BlockSpec((1, D_MODEL), const)],
        out_specs=rows(D_MODEL),
        out_shape=jax.ShapeDtypeStruct((m, D_MODEL), F32),
        compiler_params=pltpu.CompilerParams(
            dimension_semantics=("parallel",), vmem_limit_bytes=VMEM_LIMIT),
        name="outproj",
    )(oa, ob, sg, gates, x2, wpa, wpb, wo, final_g)


def kernel(x, norm_g, w_in, w_dec_up, b_dec, gla_norm_g, w_pa, w_pb, b_gate, w_o, final_g):
    batch, seq, _ = x.shape
    x2 = x.reshape(batch * seq, D_MODEL)

    wdu = jnp.pad(w_dec_up, ((0, RANK_PAD - GLA_RANK), (0, 0))).astype(BF16)

    gq, gk, gv, gg, gr, sq, sk, sv, sg, gates = _inproj(
        x2, norm_g.reshape(1, D_MODEL), _split_w_in(w_in), b_gate.reshape(1, 2 * D_MODEL), tm=512)

    oa = _gla(gq, gk, gv, gg, gr, wdu, b_dec.reshape(1, GLA_DK),
              gla_norm_g.reshape(1, GLA_HV), batch, seq)
    ob = _sb(sq, sk, sv, batch, seq, t=256)

    out = _out(oa, ob, sg, gates, x2, w_pa.astype(BF16), w_pb.astype(BF16), w_o.astype(BF16),
               final_g.reshape(1, D_MODEL), tm=512)
    return out.reshape(batch, seq, D_MODEL)
```

```python
import functools

import jax
import jax.numpy as jnp
from jax import lax
from jax.experimental import pallas as pl
from jax.experimental.pallas import tpu as pltpu

D_MODEL = 1024
GLA_HEADS = 4
GLA_HK = 128
GLA_HV = 256
GLA_DK = GLA_HEADS * GLA_HK
GLA_DV = GLA_HEADS * GLA_HV
GLA_RANK = 16
GLA_TAU = 16.0
GLA_CHUNK = 64
SB_HEADS = 8
SB_HD = 128
SB_WIDTH = SB_HEADS * SB_HD
EPS = 1e-6
LOG2E = 1.4426950408889634
SB_STAGES = 3
MASK_BIAS = -1e30
SB_ZERO_BITS = 160.0

GLA_GROUP = 4

LANES = 128
RANK_PAD = LANES
VMEM_LIMIT = 56 * 1024 * 1024

F32 = jnp.float32
BF16 = jnp.bfloat16

_NT = (((1,), (1,)), ((), ()))
_TN = (((0,), (0,)), ((), ()))


def _sigmoid(x):
    return 1.0 / (1.0 + jnp.exp(-x))


def _softplus(x):
    return jnp.maximum(x, 0.0) + jnp.log(1.0 + jnp.exp(-jnp.abs(x)))


def _inproj_gla_kernel(x_ref, ng_ref, wgq, wgk, wgv, wgg, wgr, wsq, wsk, wsv, wsg, wm, bg_ref,
                       wdu_ref, bdec_ref, gng_ref,
                       oa_o, sq_o, sk_o, sv_o, sg_o, m_o, st_ref, *, tiles_per_seq):
    c = GLA_CHUNK
    gr = GLA_GROUP * c

    @pl.when(pl.program_id(0) % tiles_per_seq == 0)
    def _():
        st_ref[...] = jnp.zeros_like(st_ref)

    x = x_ref[...]
    h = x * lax.rsqrt(jnp.mean(x * x, axis=-1, keepdims=True) + EPS) * ng_ref[...]
    hb = h.astype(BF16)

    def proj(w_ref):
        return jnp.dot(hb, w_ref[...], preferred_element_type=F32)

    code = proj(wgr).astype(BF16)
    q = proj(wgq) * (GLA_HK ** -0.5)
    k = proj(wgk)

    row = lax.broadcasted_iota(jnp.int32, (gr, gr), 0)
    col = lax.broadcasted_iota(jnp.int32, (gr, gr), 1)
    bd_tril = (row >= col) & ((row ^ col) < c)
    bd_tril_b = jnp.where(bd_tril, 1.0, 0.0).astype(BF16)

    u = jnp.dot(code, wdu_ref[...], preferred_element_type=F32) + bdec_ref[...]
    sq_o[...] = (proj(wsq) * (SB_HD ** -0.5 * LOG2E)).astype(BF16)
    la = -_softplus(-u) * (1.0 / GLA_TAU)
    la_hi = la.astype(BF16)
    la_lo = (la - la_hi.astype(F32)).astype(BF16)
    bcum = (jnp.dot(bd_tril_b, la_hi, preferred_element_type=F32)
            + jnp.dot(bd_tril_b, la_lo, preferred_element_type=F32))
    sk_o[...] = proj(wsk).astype(BF16)
    v = proj(wgv).astype(BF16)
    lasts = [bcum[(j + 1) * c - 1:(j + 1) * c, :] for j in range(GLA_GROUP)]
    blast = jnp.concatenate([jnp.broadcast_to(l, (c, GLA_DK)) for l in lasts], axis=0)
    qe = (q * jnp.exp(bcum)).astype(BF16)
    ke = (k * jnp.exp(-bcum)).astype(BF16)
    kd = (k * jnp.exp(blast - bcum)).astype(BF16)
    sv_o[...] = proj(wsv).astype(BF16)
    g = proj(wgg)
    gate = g * _sigmoid(g)

    for hd in range(GLA_HEADS):
        if hd == 1:
            g = proj(wsg)
            sg_o[...] = (g * _sigmoid(g)).astype(BF16)
        if hd == 2:
            m_o[...] = _sigmoid(proj(wm) + bg_ref[...]).astype(BF16)
        ks = slice(hd * GLA_HK, (hd + 1) * GLA_HK)
        vs = slice(hd * GLA_HV, (hd + 1) * GLA_HV)
        attn = lax.dot_general(qe[:, ks], ke[:, ks], _NT, preferred_element_type=F32)
        attn = jnp.where(bd_tril, attn, 0.0).astype(BF16)
        intra = jnp.dot(attn, v[:, vs], preferred_element_type=F32)
        st = st_ref[hd]
        for j in range(GLA_GROUP):
            rows = slice(j * c, (j + 1) * c)
            o = intra[rows] + lax.dot_general(
                qe[rows, ks], st.astype(BF16), _NT, preferred_element_type=F32)
            st = st * jnp.exp(lasts[j][:, ks]) + lax.dot_general(
                v[rows, vs], kd[rows, ks], _TN, preferred_element_type=F32)
            on = o * lax.rsqrt(jnp.mean(o * o, axis=-1, keepdims=True) + EPS) * gng_ref[...]
            oa_o[rows, vs] = (on * gate[rows, vs]).astype(BF16)
        st_ref[hd] = st


_REF_HEAD = 2 * GLA_DK + 2 * GLA_DV
_REF_TAIL = _REF_HEAD + GLA_RANK
_REF_M = _REF_TAIL + 4 * SB_WIDTH
_REF_COLS = _REF_M + 2 * D_MODEL
_PK_HEAD = 2 * D_MODEL
_PK_TAIL = _PK_HEAD + _REF_HEAD
_PK_RANK = _PK_TAIL + 4 * SB_WIDTH
_PK_COLS = _PK_RANK + RANK_PAD
_W_BLOCKS = ((GLA_DK, _PK_HEAD // GLA_DK), (GLA_DK, _PK_HEAD // GLA_DK + 1),
             (GLA_DV, (_PK_HEAD + 2 * GLA_DK) // GLA_DV), (GLA_DV, (_PK_HEAD + 2 * GLA_DK) // GLA_DV + 1),
             (RANK_PAD, _PK_RANK // RANK_PAD),
             (SB_WIDTH, _PK_TAIL // SB_WIDTH), (SB_WIDTH, _PK_TAIL // SB_WIDTH + 1),
             (SB_WIDTH, _PK_TAIL // SB_WIDTH + 2), (SB_WIDTH, _PK_TAIL // SB_WIDTH + 3),
             (2 * D_MODEL, 0))


def _pack_kernel(w_ref, o_ref):
    o_ref[:, 0:_PK_HEAD] = w_ref[:, _REF_M:_REF_COLS].astype(BF16)
    o_ref[:, _PK_HEAD:_PK_TAIL] = w_ref[:, 0:_REF_HEAD].astype(BF16)
    o_ref[:, _PK_TAIL:_PK_RANK] = w_ref[:, _REF_TAIL:_REF_M].astype(BF16)
    code = w_ref[:, _REF_HEAD:_REF_TAIL].astype(BF16)
    o_ref[:, _PK_RANK:_PK_COLS] = jnp.concatenate(
        [code, jnp.zeros((code.shape[0], RANK_PAD - GLA_RANK), BF16)], axis=1)


def _pack_w_in(w_in, tr=128):
    rows = w_in.shape[0]
    assert w_in.shape[1] == _REF_COLS
    return pl.pallas_call(
        _pack_kernel,
        grid=(rows // tr,),
        in_specs=[pl.BlockSpec((tr, _REF_COLS), lambda i: (i, 0))],
        out_specs=pl.BlockSpec((tr, _PK_COLS), lambda i: (i, 0)),
        out_shape=jax.ShapeDtypeStruct((rows, _PK_COLS), BF16),
        compiler_params=pltpu.CompilerParams(
            dimension_semantics=("parallel",), vmem_limit_bytes=VMEM_LIMIT),
        name="packw",
    )(w_in)


def _inproj_gla(x2, norm_g, w_all, b_gate, wdu, b_dec, gla_norm_g, seq):
    m = x2.shape[0]
    tm = GLA_GROUP * GLA_CHUNK
    const = lambda i: (0, 0)
    w_specs = [pl.BlockSpec((D_MODEL, n), functools.partial(lambda i, c: (0, c), c=c),
                            pipeline_mode=pl.Buffered(1))
               for n, c in _W_BLOCKS]
    in_specs = ([pl.BlockSpec((tm, D_MODEL), lambda i: (i, 0)),
                 pl.BlockSpec((1, D_MODEL), const)]
                + w_specs
                + [pl.BlockSpec((1, 2 * D_MODEL), const),
                   pl.BlockSpec((RANK_PAD, GLA_DK), const),
                   pl.BlockSpec((1, GLA_DK), const),
                   pl.BlockSpec((1, GLA_HV), const)])
    out_widths = [GLA_DV, SB_WIDTH, SB_WIDTH, SB_WIDTH, SB_WIDTH, 2 * D_MODEL]
    out_specs = [pl.BlockSpec((tm, n), lambda i: (i, 0)) for n in out_widths]
    out_shape = [jax.ShapeDtypeStruct((m, n), BF16) for n in out_widths]
    return pl.pallas_call(
        functools.partial(_inproj_gla_kernel, tiles_per_seq=seq // tm),
        grid=(m // tm,),
        in_specs=in_specs,
        out_specs=out_specs,
        out_shape=out_shape,
        scratch_shapes=[pltpu.VMEM((GLA_HEADS, GLA_HV, GLA_HK), F32)],
        compiler_params=pltpu.CompilerParams(
            dimension_semantics=("arbitrary",), vmem_limit_bytes=VMEM_LIMIT),
        name="inproj_gla",
    )(x2, norm_g, *([w_all] * len(_W_BLOCKS)), b_gate, wdu, b_dec, gla_norm_g)


def _sb_kernel(q_ref, k_ref, v_ref, o_ref,
               acc_ref, sp_ref, lb_ref, a_ref, carry_ref, run_ref, bias_ref, tri_ref, *, t):
    nq = q_ref.shape[0] // t
    row = lax.broadcasted_iota(jnp.int32, (t, t), 0)
    col = lax.broadcasted_iota(jnp.int32, (t, t), 1)
    tri_ref[...] = jnp.where(row > col, 1.0, 0.0).astype(BF16)
    bias_ref[0] = jnp.zeros((t, t), F32)
    bias_ref[1] = jnp.where(col < row, 0.0, MASK_BIAS)
    bias_ref[2] = jnp.full((t, t), MASK_BIAS, F32)
    sp_ref[...] = jnp.zeros_like(sp_ref)
    lb_ref[...] = jnp.full(lb_ref.shape, MASK_BIAS, F32)
    a_ref[...] = jnp.zeros_like(a_ref)
    carry_ref[...] = jnp.zeros_like(carry_ref)
    run_ref[...] = jnp.zeros_like(run_ref)

    def clear(r, _):
        acc_ref[pl.ds(pl.multiple_of(r * t, t), t), :] = jnp.zeros((t, SB_WIDTH), F32)
        return 0

    lax.fori_loop(0, nq, clear, 0)

    def trip(state):
        i1, kb1, i2, kb2, i3, kb3 = state
        live = i1 < nq
        ic = jnp.minimum(i1, nq - 1)
        kc = jnp.minimum(kb1, ic)
        first1 = kb1 == i1
        rows3 = pl.ds(pl.multiple_of(i3 * t, t), t)
        keys3 = pl.ds(pl.multiple_of(kb3 * t, t), t)
        rows1 = pl.ds(pl.multiple_of(ic * t, t), t)
        keys1 = pl.ds(pl.multiple_of(kc * t, t), t)
        bias = bias_ref[jnp.where(live, first1.astype(jnp.int32), 2)]
        run_min = None
        for h in range(SB_HEADS):
            hs = slice(h * SB_HD, (h + 1) * SB_HD)
            acc_ref[rows3, hs] += jnp.dot(a_ref[h], v_ref[keys3, hs],
                                          preferred_element_type=F32)
            between = jnp.dot(sp_ref[h], tri_ref[...], preferred_element_type=F32)
            c = carry_ref[h]
            x = (lb_ref[h] - between) - jnp.concatenate([c, c], axis=1)
            a_ref[h] = jnp.exp2(x).astype(BF16)
            z = lax.dot_general(q_ref[rows1, hs], k_ref[keys1, hs], _NT,
                                preferred_element_type=F32) + bias
            sp = jnp.maximum(z, 0.0) + jnp.log(1.0 + jnp.exp2(-jnp.abs(z))) * LOG2E
            sp_ref[h] = sp.astype(BF16)
            lb_ref[h] = z - sp
            carry = jnp.where(first1, 0.0, run_ref[h])
            carry_ref[h] = carry
            run = carry + jnp.sum(sp, axis=1, keepdims=True)
            run_ref[h] = run
            run_min = run if run_min is None else jnp.minimum(run_min, run)
        walk_on = live & (kb1 > 0) & (jnp.min(run_min) <= SB_ZERO_BITS)
        i_next = jnp.where(walk_on, i1, i1 + 1)
        kb_next = jnp.where(walk_on, kb1 - 1, i1 + 1)
        return i_next, kb_next, ic, kc, i2, kb2

    zero = jnp.int32(0)
    lax.while_loop(lambda s: s[0] < nq + (SB_STAGES - 1), trip,
                   (zero, zero, zero, zero, zero, zero))

    def emit(r, _):
        rows = pl.ds(pl.multiple_of(r * t, t), t)
        o_ref[rows, :] = acc_ref[rows, :].astype(BF16)
        return 0

    lax.fori_loop(0, nq, emit, 0)


def _sb(sq, sk, sv, batch, seq, t):
    spec = pl.BlockSpec((seq, SB_WIDTH), lambda b: (b, 0))
    return pl.pallas_call(
        functools.partial(_sb_kernel, t=t),
        grid=(batch,),
        in_specs=[spec, spec, spec],
        out_specs=spec,
        out_shape=jax.ShapeDtypeStruct((batch * seq, SB_WIDTH), BF16),
        scratch_shapes=[
            pltpu.VMEM((seq, SB_WIDTH), F32),
            pltpu.VMEM((SB_HEADS, t, t), BF16),
            pltpu.VMEM((SB_HEADS, t, t), F32),
            pltpu.VMEM((SB_HEADS, t, t), BF16),
            pltpu.VMEM((SB_HEADS, t, LANES), F32),
            pltpu.VMEM((SB_HEADS, t, LANES), F32),
            pltpu.VMEM((3, t, t), F32),
            pltpu.VMEM((t, t), BF16),
        ],
        compiler_params=pltpu.CompilerParams(
            dimension_semantics=("parallel",), vmem_limit_bytes=VMEM_LIMIT),
        name="stickbreak",
    )(sq, sk, sv)


def _out_kernel(oa_ref, ob_ref, sg_ref, m_ref, x_ref, wpa_ref, wpb_ref, wo_ref, fg_ref, o_ref):
    ya = jnp.dot(oa_ref[...], wpa_ref[...], preferred_element_type=F32)
    yb = jnp.dot(ob_ref[...] * sg_ref[...], wpb_ref[...], preferred_element_type=F32)
    merged = (m_ref[:, :D_MODEL].astype(F32) * ya + m_ref[:, D_MODEL:].astype(F32) * yb)
    y = x_ref[...] + jnp.dot(merged.astype(BF16), wo_ref[...], preferred_element_type=F32)
    o_ref[...] = y * lax.rsqrt(jnp.mean(y * y, axis=-1, keepdims=True) + EPS) * fg_ref[...]


def _out(oa, ob, sg, gates, x2, wpa, wpb, wo, final_g, tm):
    m = x2.shape[0]
    const = lambda i: (0, 0)
    rows = lambda n: pl.BlockSpec((tm, n), lambda i: (i, 0))
    wspec = pl.BlockSpec((D_MODEL, D_MODEL), const, pipeline_mode=pl.Buffered(1))
    return pl.pallas_call(
        _out_kernel,
        grid=(m // tm,),
        in_specs=[rows(GLA_DV), rows(SB_WIDTH), rows(SB_WIDTH), rows(2 * D_MODEL), rows(D_MODEL),
                  wspec, wspec, wspec, pl.BlockSpec((1, D_MODEL), const)],
        out_specs=rows(D_MODEL),
        out_shape=jax.ShapeDtypeStruct((m, D_MODEL), F32),
        compiler_params=pltpu.CompilerParams(
            dimension_semantics=("parallel",), vmem_limit_bytes=VMEM_LIMIT),
        name="outproj",
    )(oa, ob, sg, gates, x2, wpa, wpb, wo, final_g)


def kernel(x, norm_g, w_in, w_dec_up, b_dec, gla_norm_g, w_pa, w_pb, b_gate, w_o, final_g):
    batch, seq, _ = x.shape
    x2 = x.reshape(batch * seq, D_MODEL)

    wdu = jnp.pad(w_dec_up, ((0, RANK_PAD - GLA_RANK), (0, 0))).astype(BF16)

    oa, sq, sk, sv, sg, gates = _inproj_gla(
        x2, norm_g.reshape(1, D_MODEL), _pack_w_in(w_in), b_gate.reshape(1, 2 * D_MODEL),
        wdu, b_dec.reshape(1, GLA_DK), gla_norm_g.reshape(1, GLA_HV), seq)
    ob = _sb(sq, sk, sv, batch, seq, t=256)

    out = _out(oa, ob, sg, gates, x2, w_pa.astype(BF16), w_pb.astype(BF16), w_o.astype(BF16),
               final_g.reshape(1, D_MODEL), tm=512)
    return out.reshape(batch, seq, D_MODEL)
```

```python
import functools

import jax
import jax.numpy as jnp
from jax import lax
from jax.experimental import pallas as pl
from jax.experimental.pallas import tpu as pltpu

D_MODEL = 1024
GLA_HEADS = 4
GLA_HK = 128
GLA_HV = 256
GLA_DK = GLA_HEADS * GLA_HK
GLA_DV = GLA_HEADS * GLA_HV
GLA_RANK = 16
GLA_TAU = 16.0
GLA_CHUNK = 64
SB_HEADS = 8
SB_HD = 128
SB_WIDTH = SB_HEADS * SB_HD
EPS = 1e-6
LOG2E = 1.4426950408889634
MASK_BIAS = -1e30
SB_ZERO_BITS = 160.0

GLA_GROUP = 4

LANES = 128
MXU_COLS = 256
RANK_PAD = LANES
VMEM_LIMIT = 56 * 1024 * 1024

F32 = jnp.float32
BF16 = jnp.bfloat16

_NT = (((1,), (1,)), ((), ()))
_TN = (((0,), (0,)), ((), ()))


def _sigmoid(x):
    return 1.0 / (1.0 + jnp.exp(-x))


def _softplus(x):
    return jnp.maximum(x, 0.0) + jnp.log(1.0 + jnp.exp(-jnp.abs(x)))


def _sb_block(q, k, v, bias, carry, tri):
    z = lax.dot_general(q, k, _NT, preferred_element_type=F32)
    if bias is not None:
        z = z + bias
    sp = jnp.maximum(z, 0.0) + jnp.log(1.0 + jnp.exp2(-jnp.abs(z))) * LOG2E
    x = (z - sp) - jnp.dot(sp.astype(BF16), tri, preferred_element_type=F32)
    if carry is not None:
        x = x - jnp.concatenate([carry, carry], axis=1)
    pv = jnp.dot(jnp.exp2(x).astype(BF16), v, preferred_element_type=F32)
    return pv, jnp.sum(sp, axis=1, keepdims=True)


def _mixer_kernel(x_ref, ng_ref, wgq, wgk, wgv, wgg, wgr, wsq, wsk, wsv, wsg, wm, bg_ref,
                  wdu_ref, bdec_ref, gng_ref,
                  oa_o, ob_o, sg_o, m_o,
                  st_ref, q_scr, k_scr, v_scr, acc_ref, run_ref, *, tiles_per_seq):
    c = GLA_CHUNK
    t = GLA_GROUP * c
    i = pl.program_id(0)
    last = pl.num_programs(0) - 1
    p = jnp.maximum(i - 1, 0)

    @pl.when(i == 0)
    def _():
        q_scr[...] = jnp.zeros((t, SB_WIDTH), BF16)
        k_scr[pl.ds(0, t), :] = jnp.zeros((t, SB_WIDTH), BF16)
        v_scr[pl.ds(0, t), :] = jnp.zeros((t, SB_WIDTH), BF16)

    @pl.when(i % tiles_per_seq == 0)
    def _():
        st_ref[...] = jnp.zeros_like(st_ref)

    row = lax.broadcasted_iota(jnp.int32, (t, t), 0)
    col = lax.broadcasted_iota(jnp.int32, (t, t), 1)
    tri = jnp.where(row > col, 1.0, 0.0).astype(BF16)
    causal_bias = jnp.where(col < row, 0.0, MASK_BIAS)
    bd_tril = (row >= col) & ((row ^ col) < c)
    bd_tril_b = jnp.where(bd_tril, 1.0, 0.0).astype(BF16)

    def sb_head(hd, tile):
        qt = tile % tiles_per_seq
        rows_d = pl.ds(pl.multiple_of(qt * t, t), t)
        rows_p = pl.ds(pl.multiple_of(jnp.maximum(qt - 1, 0) * t, t), t)
        has_prev = (qt > 0).astype(F32)
        hs = slice(hd * SB_HD, (hd + 1) * SB_HD)
        qh = q_scr[:, hs]
        pv_d, rs_d = _sb_block(qh, k_scr[rows_d, hs], v_scr[rows_d, hs], causal_bias, None, tri)
        carry = jnp.broadcast_to(rs_d, (t, LANES))
        pv_p, rs_p = _sb_block(qh, k_scr[rows_p, hs], v_scr[rows_p, hs], None, carry, tri)
        acc_ref[:, hs] = pv_d + pv_p * has_prev
        run = carry + rs_p * has_prev
        run_ref[hd] = run
        return run

    def sb_rest(tile, lowest):
        qt = tile % tiles_per_seq

        def more(state):
            kb, low = state
            return (kb >= 0) & (low <= SB_ZERO_BITS)

        def walk(state):
            kb, _ = state
            keys = pl.ds(pl.multiple_of(kb * t, t), t)
            tri_w = jnp.where(row > col, 1.0, 0.0).astype(BF16)
            low = None
            for hd in range(SB_HEADS):
                hs = slice(hd * SB_HD, (hd + 1) * SB_HD)
                carry = run_ref[hd]
                pv, rs = _sb_block(q_scr[:, hs], k_scr[keys, hs], v_scr[keys, hs],
                                   None, carry, tri_w)
                acc_ref[:, hs] += pv
                run = carry + rs
                run_ref[hd] = run
                low = run if low is None else jnp.minimum(low, run)
            return kb - 1, jnp.min(low)

        lax.while_loop(more, walk, (qt - 2, lowest))
        ob_o[pl.ds(pl.multiple_of((tile % 2) * t, t), t), :] = acc_ref[...].astype(BF16)

    def gla_head(gh, qe, ke, kd, v, gate, lasts):
        ks = slice(gh * GLA_HK, (gh + 1) * GLA_HK)
        vs = slice(gh * GLA_HV, (gh + 1) * GLA_HV)
        attn = lax.dot_general(qe[:, ks], ke[:, ks], _NT, preferred_element_type=F32)
        attn = jnp.where(bd_tril, attn, 0.0).astype(BF16)
        intra = jnp.dot(attn, v[:, vs], preferred_element_type=F32)
        st = st_ref[gh]
        for j in range(GLA_GROUP):
            rows = slice(j * c, (j + 1) * c)
            o = intra[rows] + lax.dot_general(
                qe[rows, ks], st.astype(BF16), _NT, preferred_element_type=F32)
            st = st * jnp.exp(lasts[j][:, ks]) + lax.dot_general(
                v[rows, vs], kd[rows, ks], _TN, preferred_element_type=F32)
            on = o * lax.rsqrt(jnp.mean(o * o, axis=-1, keepdims=True) + EPS) * gng_ref[...]
            oa_o[rows, vs] = (on * gate[rows, vs]).astype(BF16)
        st_ref[gh] = st

    x = x_ref[...]
    h = x * lax.rsqrt(jnp.mean(x * x, axis=-1, keepdims=True) + EPS) * ng_ref[...]
    hb = h.astype(BF16)

    def proj(w_ref):
        return jnp.dot(hb, w_ref[...], preferred_element_type=F32)

    cur = {}

    half = SB_WIDTH // 2

    def piece_sq(j):
        cols = slice(j * half, (j + 1) * half)
        def run_piece():
            cur["sq", j] = (proj(wsq.at[:, cols]) * (SB_HD ** -0.5 * LOG2E)).astype(BF16)
        return run_piece

    def piece_skv(name, w_ref, j):
        cols = slice(j * half, (j + 1) * half)
        def run_piece():
            cur[name, j] = proj(w_ref.at[:, cols]).astype(BF16)
        return run_piece

    def piece_gla_in():
        cur["code"] = proj(wgr).astype(BF16)
        cur["q"] = proj(wgq) * (GLA_HK ** -0.5)
        cur["k"] = proj(wgk)

    def piece_decay():
        u = jnp.dot(cur["code"], wdu_ref[...], preferred_element_type=F32) + bdec_ref[...]
        la = -_softplus(-u) * (1.0 / GLA_TAU)
        la_hi = la.astype(BF16)
        la_lo = (la - la_hi.astype(F32)).astype(BF16)
        cur["bcum"] = (jnp.dot(bd_tril_b, la_hi, preferred_element_type=F32)
                       + jnp.dot(bd_tril_b, la_lo, preferred_element_type=F32))

    def piece_qkd():
        bcum = cur["bcum"]
        cur["v"] = proj(wgv).astype(BF16)
        cur["lasts"] = [bcum[(j + 1) * c - 1:(j + 1) * c, :] for j in range(GLA_GROUP)]
        blast = jnp.concatenate([jnp.broadcast_to(l, (c, GLA_DK)) for l in cur["lasts"]], axis=0)
        cur["qe"] = (cur["q"] * jnp.exp(bcum)).astype(BF16)
        cur["ke"] = (cur["k"] * jnp.exp(-bcum)).astype(BF16)
        cur["kd"] = (cur["k"] * jnp.exp(blast - bcum)).astype(BF16)

    def piece_gate():
        g = proj(wgg)
        cur["gate"] = g * _sigmoid(g)

    def piece_gla(gh):
        return lambda: gla_head(gh, cur["qe"], cur["ke"], cur["kd"], cur["v"], cur["gate"],
                                cur["lasts"])

    def piece_sg(j):
        cols = slice(j * half, (j + 1) * half)
        def run_piece():
            g = proj(wsg.at[:, cols])
            sg_o[:, cols] = (g * _sigmoid(g)).astype(BF16)
        return run_piece

    def piece_m(j):
        cols = slice(j * half, (j + 1) * half)
        def run_piece():
            m_o[:, cols] = _sigmoid(proj(wm.at[:, cols]) + bg_ref[:, cols]).astype(BF16)
        return run_piece

    pieces = [piece_gla_in, piece_decay, piece_qkd, piece_gate,
              piece_gla(0), piece_sq(0), piece_gla(1), piece_sq(1),
              piece_gla(2), piece_skv("sk", wsk, 0), piece_gla(3), piece_skv("sk", wsk, 1),
              piece_skv("sv", wsv, 0), piece_skv("sv", wsv, 1), piece_sg(0), piece_sg(1),
              lambda: (piece_m(0)(), piece_m(1)()), lambda: (piece_m(2)(), piece_m(3)())]

    qt_p = p % tiles_per_seq
    rows_d = pl.ds(pl.multiple_of(qt_p * t, t), t)
    rows_b = pl.ds(pl.multiple_of(jnp.maximum(qt_p - 1, 0) * t, t), t)
    has_prev = (qt_p > 0).astype(F32)
    n_blocks = 2 * SB_HEADS
    blk = [dict() for _ in range(n_blocks)]

    def stage1(n):
        hd, before = divmod(n, 2)
        hs = slice(hd * SB_HD, (hd + 1) * SB_HD)
        rows = rows_b if before else rows_d
        z = lax.dot_general(q_scr[:, hs], k_scr[rows, hs], _NT, preferred_element_type=F32)
        if not before:
            z = z + causal_bias
        sp = jnp.maximum(z, 0.0) + jnp.log(1.0 + jnp.exp2(-jnp.abs(z))) * LOG2E
        blk[n].update(sp=sp.astype(BF16), lb=z - sp, rs=jnp.sum(sp, axis=1, keepdims=True))

    def stage2(n):
        x = blk[n]["lb"] - jnp.dot(blk[n]["sp"], tri, preferred_element_type=F32)
        if n % 2:
            carry = jnp.broadcast_to(blk[n - 1]["rs"], (t, LANES))
            x = x - jnp.concatenate([carry, carry], axis=1)
        blk[n]["a"] = jnp.exp2(x).astype(BF16)

    def stage3(n):
        hd, before = divmod(n, 2)
        hs = slice(hd * SB_HD, (hd + 1) * SB_HD)
        rows = rows_b if before else rows_d
        blk[n]["pv"] = jnp.dot(blk[n]["a"], v_scr[rows, hs], preferred_element_type=F32)
        if before:
            acc_ref[:, hs] = blk[n - 1]["pv"] + blk[n]["pv"] * has_prev
            run = jnp.broadcast_to(blk[n - 1]["rs"] + blk[n]["rs"] * has_prev, (t, LANES))
            run_ref[hd] = run
            blk[n]["run"] = run

    for n in range(n_blocks + 2):
        if 0 <= n - 2 < n_blocks:
            stage3(n - 2)
        if 0 <= n - 1 < n_blocks:
            stage2(n - 1)
        if n < n_blocks:
            stage1(n)
        if n < len(pieces):
            pieces[n]()
    run_min = blk[1]["run"]
    for hd in range(1, SB_HEADS):
        run_min = jnp.minimum(run_min, blk[2 * hd + 1]["run"])

    sb_rest(p, jnp.min(run_min))

    rows_now = pl.ds(pl.multiple_of((i % tiles_per_seq) * t, t), t)
    for j in range(2):
        cols = slice(j * half, (j + 1) * half)
        q_scr[:, cols] = cur["sq", j]
        k_scr[rows_now, cols] = cur["sk", j]
        v_scr[rows_now, cols] = cur["sv", j]

    @pl.when(i == last)
    def _():
        low = sb_head(0, i)
        for hd in range(1, SB_HEADS):
            low = jnp.minimum(low, sb_head(hd, i))
        sb_rest(i, jnp.min(low))


_REF_HEAD = 2 * GLA_DK + 2 * GLA_DV
_REF_TAIL = _REF_HEAD + GLA_RANK
_REF_M = _REF_TAIL + 4 * SB_WIDTH
_REF_COLS = _REF_M + 2 * D_MODEL
_PK_HEAD = 2 * D_MODEL
_PK_TAIL = _PK_HEAD + _REF_HEAD
_PK_RANK = _PK_TAIL + 4 * SB_WIDTH
_PK_COLS = _PK_RANK + RANK_PAD
_W_BLOCKS = ((GLA_DK, _PK_HEAD // GLA_DK), (GLA_DK, _PK_HEAD // GLA_DK + 1),
             (GLA_DV, (_PK_HEAD + 2 * GLA_DK) // GLA_DV), (GLA_DV, (_PK_HEAD + 2 * GLA_DK) // GLA_DV + 1),
             (RANK_PAD, _PK_RANK // RANK_PAD),
             (SB_WIDTH, _PK_TAIL // SB_WIDTH), (SB_WIDTH, _PK_TAIL // SB_WIDTH + 1),
             (SB_WIDTH, _PK_TAIL // SB_WIDTH + 2), (SB_WIDTH, _PK_TAIL // SB_WIDTH + 3),
             (2 * D_MODEL, 0))


def _pack_kernel(w_ref, o_ref):
    o_ref[:, 0:_PK_HEAD] = w_ref[:, _REF_M:_REF_COLS].astype(BF16)
    o_ref[:, _PK_HEAD:_PK_TAIL] = w_ref[:, 0:_REF_HEAD].astype(BF16)
    o_ref[:, _PK_TAIL:_PK_RANK] = w_ref[:, _REF_TAIL:_REF_M].astype(BF16)
    code = w_ref[:, _REF_HEAD:_REF_TAIL].astype(BF16)
    o_ref[:, _PK_RANK:_PK_COLS] = jnp.concatenate(
        [code, jnp.zeros((code.shape[0], RANK_PAD - GLA_RANK), BF16)], axis=1)


def _pack_w_in(w_in, tr=128):
    rows = w_in.shape[0]
    assert w_in.shape[1] == _REF_COLS
    return pl.pallas_call(
        _pack_kernel,
        grid=(rows // tr,),
        in_specs=[pl.BlockSpec((tr, _REF_COLS), lambda i: (i, 0))],
        out_specs=pl.BlockSpec((tr, _PK_COLS), lambda i: (i, 0)),
        out_shape=jax.ShapeDtypeStruct((rows, _PK_COLS), BF16),
        compiler_params=pltpu.CompilerParams(
            dimension_semantics=("parallel",), vmem_limit_bytes=VMEM_LIMIT),
        name="packw",
    )(w_in)


def _mixers(x2, norm_g, w_all, b_gate, wdu, b_dec, gla_norm_g, seq):
    m = x2.shape[0]
    tm = GLA_GROUP * GLA_CHUNK
    const = lambda i: (0, 0)
    w_specs = [pl.BlockSpec((D_MODEL, n), functools.partial(lambda i, c: (0, c), c=c),
                            pipeline_mode=pl.Buffered(1))
               for n, c in _W_BLOCKS]
    in_specs = ([pl.BlockSpec((tm, D_MODEL), lambda i: (i, 0)),
                 pl.BlockSpec((1, D_MODEL), const)]
                + w_specs
                + [pl.BlockSpec((1, 2 * D_MODEL), const),
                   pl.BlockSpec((RANK_PAD, GLA_DK), const),
                   pl.BlockSpec((1, GLA_DK), const),
                   pl.BlockSpec((1, GLA_HV), const)])
    out_widths = [GLA_DV, SB_WIDTH, SB_WIDTH, 2 * D_MODEL]
    out_specs = [pl.BlockSpec((tm, n), lambda i: (i, 0)) for n in out_widths]
    out_specs[1] = pl.BlockSpec((2 * tm, SB_WIDTH), lambda i: (jnp.maximum(i - 1, 0) // 2, 0))
    out_shape = [jax.ShapeDtypeStruct((m, n), BF16) for n in out_widths]
    return pl.pallas_call(
        functools.partial(_mixer_kernel, tiles_per_seq=seq // tm),
        grid=(m // tm,),
        in_specs=in_specs,
        out_specs=out_specs,
        out_shape=out_shape,
        scratch_shapes=[
            pltpu.VMEM((GLA_HEADS, GLA_HV, GLA_HK), F32),
            pltpu.VMEM((tm, SB_WIDTH), BF16),
            pltpu.VMEM((seq, SB_WIDTH), BF16),
            pltpu.VMEM((seq, SB_WIDTH), BF16),
            pltpu.VMEM((tm, SB_WIDTH), F32),
            pltpu.VMEM((SB_HEADS, tm, LANES), F32),
        ],
        compiler_params=pltpu.CompilerParams(
            dimension_semantics=("arbitrary",), vmem_limit_bytes=VMEM_LIMIT),
        name="mixers",
    )(x2, norm_g, *([w_all] * len(_W_BLOCKS)), b_gate, wdu, b_dec, gla_norm_g)


def _out_kernel(oa_ref, ob_ref, sg_ref, m_ref, x_ref, wpa_ref, wpb_ref, wo_ref, fg_ref, o_ref):
    ya = jnp.dot(oa_ref[...], wpa_ref[...], preferred_element_type=F32)
    yb = jnp.dot(ob_ref[...] * sg_ref[...], wpb_ref[...], preferred_element_type=F32)
    merged = (m_ref[:, :D_MODEL].astype(F32) * ya + m_ref[:, D_MODEL:].astype(F32) * yb)
    y = x_ref[...] + jnp.dot(merged.astype(BF16), wo_ref[...], preferred_element_type=F32)
    o_ref[...] = y * lax.rsqrt(jnp.mean(y * y, axis=-1, keepdims=True) + EPS) * fg_ref[...]


def _out(oa, ob, sg, gates, x2, wpa, wpb, wo, final_g, tm):
    m = x2.shape[0]
    const = lambda i: (0, 0)
    rows = lambda n: pl.BlockSpec((tm, n), lambda i: (i, 0))
    wspec = pl.BlockSpec((D_MODEL, D_MODEL), const, pipeline_mode=pl.Buffered(1))
    return pl.pallas_call(
        _out_kernel,
        grid=(m // tm,),
        in_specs=[rows(GLA_DV), rows(SB_WIDTH), rows(SB_WIDTH), rows(2 * D_MODEL), rows(D_MODEL),
                  wspec, wspec, wspec, pl.BlockSpec((1, D_MODEL), const)],
        out_specs=rows(D_MODEL),
        out_shape=jax.ShapeDtypeStruct((m, D_MODEL), F32),
        compiler_params=pltpu.CompilerParams(
            dimension_semantics=("parallel",), vmem_limit_bytes=VMEM_LIMIT),
        name="outproj",
    )(oa, ob, sg, gates, x2, wpa, wpb, wo, final_g)


def kernel(x, norm_g, w_in, w_dec_up, b_dec, gla_norm_g, w_pa, w_pb, b_gate, w_o, final_g):
    batch, seq, _ = x.shape
    x2 = x.reshape(batch * seq, D_MODEL)

    wdu = jnp.pad(w_dec_up, ((0, RANK_PAD - GLA_RANK), (0, 0))).astype(BF16)

    oa, ob, sg, gates = _mixers(
        x2, norm_g.reshape(1, D_MODEL), _pack_w_in(w_in), b_gate.reshape(1, 2 * D_MODEL),
        wdu, b_dec.reshape(1, GLA_DK), gla_norm_g.reshape(1, GLA_HV), seq)

    out = _out(oa, ob, sg, gates, x2, w_pa.astype(BF16), w_pb.astype(BF16), w_o.astype(BF16),
               final_g.reshape(1, D_MODEL), tm=512)
    return out.reshape(batch, seq, D_MODEL)
```

```python
import functools

import jax
import jax.numpy as jnp
from jax import lax
from jax.experimental import pallas as pl
from jax.experimental.pallas import tpu as pltpu

D_MODEL = 1024
GLA_HEADS = 4
GLA_HK = 128
GLA_HV = 256
GLA_DK = GLA_HEADS * GLA_HK
GLA_DV = GLA_HEADS * GLA_HV
GLA_RANK = 16
GLA_TAU = 16.0
GLA_CHUNK = 64
SB_HEADS = 8
SB_HD = 128
SB_WIDTH = SB_HEADS * SB_HD
EPS = 1e-6
LOG2E = 1.4426950408889634
MASK_BIAS = -1e30
SB_ZERO_BITS = 160.0

GLA_GROUP = 4

LANES = 128
MXU_COLS = 256
RANK_PAD = MXU_COLS
VMEM_LIMIT = 56 * 1024 * 1024

F32 = jnp.float32
BF16 = jnp.bfloat16

_NT = (((1,), (1,)), ((), ()))
_TN = (((0,), (0,)), ((), ()))


def _sigmoid(x):
    return 1.0 / (1.0 + jnp.exp(-x))


def _softplus(x):
    return jnp.maximum(x, 0.0) + jnp.log(1.0 + jnp.exp(-jnp.abs(x)))


def _sb_block(q, k, v, bias, carry, tri):
    z = lax.dot_general(q, k, _NT, preferred_element_type=F32)
    if bias is not None:
        z = z + bias
    sp = jnp.maximum(z, 0.0) + jnp.log(1.0 + jnp.exp2(-jnp.abs(z))) * LOG2E
    x = (z - sp) - jnp.dot(sp.astype(BF16), tri, preferred_element_type=F32)
    if carry is not None:
        x = x - jnp.concatenate([carry, carry], axis=1)
    pv = jnp.dot(jnp.exp2(x).astype(BF16), v, preferred_element_type=F32)
    return pv, jnp.sum(sp, axis=1, keepdims=True)


def _mixer_kernel(x_ref, ng_ref, wgq, wgk, wgv, wgg, wgr, wsq, wsk, wsv, wsg, wm, bg_ref,
                  wdu_ref, bdec_ref, gng_ref,
                  oa_o, ob_o, sg_o, m_o,
                  st_ref, q_scr, k_scr, v_scr, acc_ref, run_ref, *, tiles_per_seq):
    c = GLA_CHUNK
    t = GLA_GROUP * c
    i = pl.program_id(0)
    last = pl.num_programs(0) - 1
    p = jnp.maximum(i - 1, 0)

    @pl.when(i == 0)
    def _():
        q_scr[...] = jnp.zeros((t, SB_WIDTH), BF16)
        k_scr[pl.ds(0, t), :] = jnp.zeros((t, SB_WIDTH), BF16)
        v_scr[pl.ds(0, t), :] = jnp.zeros((t, SB_WIDTH), BF16)

    @pl.when(i % tiles_per_seq == 0)
    def _():
        st_ref[...] = jnp.zeros_like(st_ref)

    row = lax.broadcasted_iota(jnp.int32, (t, t), 0)
    col = lax.broadcasted_iota(jnp.int32, (t, t), 1)
    tri = jnp.where(row > col, 1.0, 0.0).astype(BF16)
    causal_bias = jnp.where(col < row, 0.0, MASK_BIAS)
    bd_tril = (row >= col) & ((row ^ col) < c)
    bd_tril_b = jnp.where(bd_tril, 1.0, 0.0).astype(BF16)

    def sb_head(hd, tile):
        qt = tile % tiles_per_seq
        rows_d = pl.ds(pl.multiple_of(qt * t, t), t)
        rows_p = pl.ds(pl.multiple_of(jnp.maximum(qt - 1, 0) * t, t), t)
        has_prev = (qt > 0).astype(F32)
        hs = slice(hd * SB_HD, (hd + 1) * SB_HD)
        qh = q_scr[:, hs]
        pv_d, rs_d = _sb_block(qh, k_scr[rows_d, hs], v_scr[rows_d, hs], causal_bias, None, tri)
        carry = jnp.broadcast_to(rs_d, (t, LANES))
        pv_p, rs_p = _sb_block(qh, k_scr[rows_p, hs], v_scr[rows_p, hs], None, carry, tri)
        acc_ref[:, hs] = pv_d + pv_p * has_prev
        run = carry + rs_p * has_prev
        run_ref[hd] = run
        return run

    def sb_rest(tile, lowest):
        qt = tile % tiles_per_seq

        def more(state):
            kb, low = state
            return (kb >= 0) & (low <= SB_ZERO_BITS)

        def walk(state):
            kb, _ = state
            keys = pl.ds(pl.multiple_of(kb * t, t), t)
            tri_w = jnp.where(row > col, 1.0, 0.0).astype(BF16)
            low = None
            for hd in range(SB_HEADS):
                hs = slice(hd * SB_HD, (hd + 1) * SB_HD)
                carry = run_ref[hd]
                pv, rs = _sb_block(q_scr[:, hs], k_scr[keys, hs], v_scr[keys, hs],
                                   None, carry, tri_w)
                acc_ref[:, hs] += pv
                run = carry + rs
                run_ref[hd] = run
                low = run if low is None else jnp.minimum(low, run)
            return kb - 1, jnp.min(low)

        lax.while_loop(more, walk, (qt - 2, lowest))
        ob_o[pl.ds(pl.multiple_of((tile % 2) * t, t), t), :] = acc_ref[...].astype(BF16)

    def gla_head(gh, qe, ke, kd, v, gate, lasts):
        ks = slice(gh * GLA_HK, (gh + 1) * GLA_HK)
        vs = slice(gh * GLA_HV, (gh + 1) * GLA_HV)
        attn = lax.dot_general(qe[:, ks], ke[:, ks], _NT, preferred_element_type=F32)
        attn = jnp.where(bd_tril, attn, 0.0).astype(BF16)
        intra = jnp.dot(attn, v[:, vs], preferred_element_type=F32)
        st = st_ref[gh]
        for j in range(GLA_GROUP):
            rows = slice(j * c, (j + 1) * c)
            o = intra[rows] + lax.dot_general(
                qe[rows, ks], st.astype(BF16), _NT, preferred_element_type=F32)
            st = st * jnp.exp(lasts[j][:, ks]) + lax.dot_general(
                v[rows, vs], kd[rows, ks], _TN, preferred_element_type=F32)
            on = o * lax.rsqrt(jnp.mean(o * o, axis=-1, keepdims=True) + EPS) * gng_ref[...]
            oa_o[rows, vs] = (on * gate[rows, vs]).astype(BF16)
        st_ref[gh] = st

    x = x_ref[...]
    h = x * lax.rsqrt(jnp.mean(x * x, axis=-1, keepdims=True) + EPS) * ng_ref[...]
    hb = h.astype(BF16)

    def proj(w_ref):
        return jnp.dot(hb, w_ref[...], preferred_element_type=F32)

    cur = {}

    half = SB_WIDTH // 2

    def piece_sq(j):
        cols = slice(j * half, (j + 1) * half)
        def run_piece():
            cur["sq", j] = (proj(wsq.at[:, cols]) * (SB_HD ** -0.5 * LOG2E)).astype(BF16)
        return run_piece

    def piece_skv(name, w_ref, j):
        cols = slice(j * half, (j + 1) * half)
        def run_piece():
            cur[name, j] = proj(w_ref.at[:, cols]).astype(BF16)
        return run_piece

    def piece_gla_in():
        cur["code"] = proj(wgr).astype(BF16)
        cur["q"] = proj(wgq) * (GLA_HK ** -0.5)
        cur["k"] = proj(wgk)

    def piece_decay():
        u = jnp.dot(cur["code"], wdu_ref[...], preferred_element_type=F32) + bdec_ref[...]
        la = -_softplus(-u) * (1.0 / GLA_TAU)
        la_hi = la.astype(BF16)
        la_lo = (la - la_hi.astype(F32)).astype(BF16)
        cur["bcum"] = (jnp.dot(bd_tril_b, la_hi, preferred_element_type=F32)
                       + jnp.dot(bd_tril_b, la_lo, preferred_element_type=F32))

    def piece_qkd():
        bcum = cur["bcum"]
        cur["v"] = proj(wgv).astype(BF16)
        cur["lasts"] = [bcum[(j + 1) * c - 1:(j + 1) * c, :] for j in range(GLA_GROUP)]
        blast = jnp.concatenate([jnp.broadcast_to(l, (c, GLA_DK)) for l in cur["lasts"]], axis=0)
        cur["qe"] = (cur["q"] * jnp.exp(bcum)).astype(BF16)
        cur["ke"] = (cur["k"] * jnp.exp(-bcum)).astype(BF16)
        cur["kd"] = (cur["k"] * jnp.exp(blast - bcum)).astype(BF16)

    def piece_gate():
        g = proj(wgg)
        cur["gate"] = g * _sigmoid(g)

    def piece_gla(gh):
        return lambda: gla_head(gh, cur["qe"], cur["ke"], cur["kd"], cur["v"], cur["gate"],
                                cur["lasts"])

    def piece_sg(j):
        cols = slice(j * half, (j + 1) * half)
        def run_piece():
            g = proj(wsg.at[:, cols])
            sg_o[:, cols] = (g * _sigmoid(g)).astype(BF16)
        return run_piece

    def piece_m(j):
        cols = slice(j * half, (j + 1) * half)
        def run_piece():
            m_o[:, cols] = _sigmoid(proj(wm.at[:, cols]) + bg_ref[:, cols]).astype(BF16)
        return run_piece

    pieces = [piece_gla_in, piece_decay, piece_qkd, piece_gate,
              piece_gla(0), piece_sq(0), piece_gla(1), piece_sq(1),
              piece_gla(2), piece_skv("sk", wsk, 0), piece_gla(3), piece_skv("sk", wsk, 1),
              piece_skv("sv", wsv, 0), piece_skv("sv", wsv, 1), piece_sg(0), piece_sg(1),
              lambda: (piece_m(0)(), piece_m(1)()), lambda: (piece_m(2)(), piece_m(3)())]

    qt_p = p % tiles_per_seq
    rows_d = pl.ds(pl.multiple_of(qt_p * t, t), t)
    rows_b = pl.ds(pl.multiple_of(jnp.maximum(qt_p - 1, 0) * t, t), t)
    has_prev = (qt_p > 0).astype(F32)
    n_blocks = 2 * SB_HEADS
    blk = [dict() for _ in range(n_blocks)]

    def stage1(n):
        hd, before = divmod(n, 2)
        hs = slice(hd * SB_HD, (hd + 1) * SB_HD)
        rows = rows_b if before else rows_d
        z = lax.dot_general(q_scr[:, hs], k_scr[rows, hs], _NT, preferred_element_type=F32)
        if not before:
            z = z + causal_bias
        sp = jnp.maximum(z, 0.0) + jnp.log(1.0 + jnp.exp2(-jnp.abs(z))) * LOG2E
        blk[n].update(sp=sp.astype(BF16), lb=z - sp, rs=jnp.sum(sp, axis=1, keepdims=True))

    def stage2(n):
        x = blk[n]["lb"] - jnp.dot(blk[n]["sp"], tri, preferred_element_type=F32)
        if n % 2:
            carry = jnp.broadcast_to(blk[n - 1]["rs"], (t, LANES))
            x = x - jnp.concatenate([carry, carry], axis=1)
        blk[n]["a"] = jnp.exp2(x).astype(BF16)

    def stage3(n):
        hd, before = divmod(n, 2)
        hs = slice(hd * SB_HD, (hd + 1) * SB_HD)
        rows = rows_b if before else rows_d
        blk[n]["pv"] = jnp.dot(blk[n]["a"], v_scr[rows, hs], preferred_element_type=F32)
        if before:
            acc_ref[:, hs] = blk[n - 1]["pv"] + blk[n]["pv"] * has_prev
            run = jnp.broadcast_to(blk[n - 1]["rs"] + blk[n]["rs"] * has_prev, (t, LANES))
            run_ref[hd] = run
            blk[n]["run"] = run

    for n in range(n_blocks + 2):
        if 0 <= n - 2 < n_blocks:
            stage3(n - 2)
        if 0 <= n - 1 < n_blocks:
            stage2(n - 1)
        if n < n_blocks:
            stage1(n)
        if n < len(pieces):
            pieces[n]()
    run_min = blk[1]["run"]
    for hd in range(1, SB_HEADS):
        run_min = jnp.minimum(run_min, blk[2 * hd + 1]["run"])

    sb_rest(p, jnp.min(run_min))

    rows_now = pl.ds(pl.multiple_of((i % tiles_per_seq) * t, t), t)
    for j in range(2):
        cols = slice(j * half, (j + 1) * half)
        q_scr[:, cols] = cur["sq", j]
        k_scr[rows_now, cols] = cur["sk", j]
        v_scr[rows_now, cols] = cur["sv", j]

    @pl.when(i == last)
    def _():
        low = sb_head(0, i)
        for hd in range(1, SB_HEADS):
            low = jnp.minimum(low, sb_head(hd, i))
        sb_rest(i, jnp.min(low))


_REF_HEAD = 2 * GLA_DK + 2 * GLA_DV
_REF_TAIL = _REF_HEAD + GLA_RANK
_REF_M = _REF_TAIL + 4 * SB_WIDTH
_REF_COLS = _REF_M + 2 * D_MODEL
_PK_HEAD = 2 * D_MODEL
_PK_TAIL = _PK_HEAD + _REF_HEAD
_PK_RANK = _PK_TAIL + 4 * SB_WIDTH
_PK_COLS = _PK_RANK + RANK_PAD
_W_BLOCKS = ((GLA_DK, _PK_HEAD // GLA_DK), (GLA_DK, _PK_HEAD // GLA_DK + 1),
             (GLA_DV, (_PK_HEAD + 2 * GLA_DK) // GLA_DV), (GLA_DV, (_PK_HEAD + 2 * GLA_DK) // GLA_DV + 1),
             (RANK_PAD, _PK_RANK // RANK_PAD),
             (SB_WIDTH, _PK_TAIL // SB_WIDTH), (SB_WIDTH, _PK_TAIL // SB_WIDTH + 1),
             (SB_WIDTH, _PK_TAIL // SB_WIDTH + 2), (SB_WIDTH, _PK_TAIL // SB_WIDTH + 3),
             (2 * D_MODEL, 0))


def _pack_kernel(src_ref, wt_ref, o_ref):
    j = pl.program_id(0)
    w = wt_ref[...]
    n_row = lax.broadcasted_iota(jnp.int32, w.shape, 0)
    is_code = j == pl.num_programs(0) - 1
    w = jnp.where(is_code & (n_row >= GLA_RANK), 0.0, w).astype(BF16)
    r = lax.broadcasted_iota(jnp.int32, (MXU_COLS, MXU_COLS), 0)
    cidx = lax.broadcasted_iota(jnp.int32, (MXU_COLS, MXU_COLS), 1)
    eye = jnp.where(r == cidx, 1.0, 0.0).astype(BF16)
    for kk in range(0, w.shape[1], MXU_COLS):
        o_ref[kk:kk + MXU_COLS, :] = lax.dot_general(
            eye, w[:, kk:kk + MXU_COLS], _NT, preferred_element_type=F32).astype(BF16)


def _pack_w_in(w_in):
    d = w_in.shape[0]
    assert w_in.shape[1] == _REF_COLS and _PK_COLS % MXU_COLS == 0
    src = ([_REF_M + c for c in range(0, _PK_HEAD, MXU_COLS)]
           + [c for c in range(0, _REF_HEAD, MXU_COLS)]
           + [_REF_TAIL + c for c in range(0, _REF_M - _REF_TAIL, MXU_COLS)]
           + [_REF_HEAD])
    assert len(src) == _PK_COLS // MXU_COLS
    assert all(c % GLA_RANK == 0 for c in src)
    src = [c // GLA_RANK for c in src]
    grid_spec = pltpu.PrefetchScalarGridSpec(
        num_scalar_prefetch=1,
        grid=(len(src),),
        in_specs=[pl.BlockSpec((pl.Element(MXU_COLS), pl.Element(d)),
                               lambda j, src_ref: (src_ref[j] * GLA_RANK, 0))],
        out_specs=pl.BlockSpec((d, MXU_COLS), lambda j, src_ref: (0, j)))
    return pl.pallas_call(
        _pack_kernel,
        grid_spec=grid_spec,
        out_shape=jax.ShapeDtypeStruct((d, _PK_COLS), BF16),
        compiler_params=pltpu.CompilerParams(
            dimension_semantics=("parallel",), vmem_limit_bytes=VMEM_LIMIT),
        name="packw",
    )(jnp.asarray(src, jnp.int32), w_in.T)


def _mixers(x2, norm_g, w_all, b_gate, wdu, b_dec, gla_norm_g, seq):
    m = x2.shape[0]
    tm = GLA_GROUP * GLA_CHUNK
    const = lambda i: (0, 0)
    w_specs = [pl.BlockSpec((D_MODEL, n), functools.partial(lambda i, c: (0, c), c=c),
                            pipeline_mode=pl.Buffered(1))
               for n, c in _W_BLOCKS]
    in_specs = ([pl.BlockSpec((tm, D_MODEL), lambda i: (i, 0)),
                 pl.BlockSpec((1, D_MODEL), const)]
                + w_specs
                + [pl.BlockSpec((1, 2 * D_MODEL), const),
                   pl.BlockSpec((RANK_PAD, GLA_DK), const),
                   pl.BlockSpec((1, GLA_DK), const),
                   pl.BlockSpec((1, GLA_HV), const)])
    out_widths = [GLA_DV, SB_WIDTH, SB_WIDTH, 2 * D_MODEL]
    out_specs = [pl.BlockSpec((tm, n), lambda i: (i, 0)) for n in out_widths]
    out_specs[1] = pl.BlockSpec((2 * tm, SB_WIDTH), lambda i: (jnp.maximum(i - 1, 0) // 2, 0))
    out_shape = [jax.ShapeDtypeStruct((m, n), BF16) for n in out_widths]
    return pl.pallas_call(
        functools.partial(_mixer_kernel, tiles_per_seq=seq // tm),
        grid=(m // tm,),
        in_specs=in_specs,
        out_specs=out_specs,
        out_shape=out_shape,
        scratch_shapes=[
            pltpu.VMEM((GLA_HEADS, GLA_HV, GLA_HK), F32),
            pltpu.VMEM((tm, SB_WIDTH), BF16),
            pltpu.VMEM((seq, SB_WIDTH), BF16),
            pltpu.VMEM((seq, SB_WIDTH), BF16),
            pltpu.VMEM((tm, SB_WIDTH), F32),
            pltpu.VMEM((SB_HEADS, tm, LANES), F32),
        ],
        compiler_params=pltpu.CompilerParams(
            dimension_semantics=("arbitrary",), vmem_limit_bytes=VMEM_LIMIT),
        name="mixers",
    )(x2, norm_g, *([w_all] * len(_W_BLOCKS)), b_gate, wdu, b_dec, gla_norm_g)


def _out_kernel(oa_ref, ob_ref, sg_ref, m_ref, x_ref, wpa_ref, wpb_ref, wo_ref, fg_ref, o_ref,
                wpa_b, wpb_b, wo_b):
    @pl.when(pl.program_id(0) == 0)
    def _():
        wpa_b[...] = wpa_ref[...].astype(BF16)
        wpb_b[...] = wpb_ref[...].astype(BF16)
        wo_b[...] = wo_ref[...].astype(BF16)

    ya = jnp.dot(oa_ref[...], wpa_b[...], preferred_element_type=F32)
    yb = jnp.dot(ob_ref[...] * sg_ref[...], wpb_b[...], preferred_element_type=F32)
    merged = (m_ref[:, :D_MODEL].astype(F32) * ya + m_ref[:, D_MODEL:].astype(F32) * yb)
    y = x_ref[...] + jnp.dot(merged.astype(BF16), wo_b[...], preferred_element_type=F32)
    o_ref[...] = y * lax.rsqrt(jnp.mean(y * y, axis=-1, keepdims=True) + EPS) * fg_ref[...]


def _out(oa, ob, sg, gates, x2, wpa, wpb, wo, final_g, tm):
    m = x2.shape[0]
    const = lambda i: (0, 0)
    rows = lambda n: pl.BlockSpec((tm, n), lambda i: (i, 0))
    wspec = pl.BlockSpec((D_MODEL, D_MODEL), const, pipeline_mode=pl.Buffered(1))
    return pl.pallas_call(
        _out_kernel,
        grid=(m // tm,),
        in_specs=[rows(GLA_DV), rows(SB_WIDTH), rows(SB_WIDTH), rows(2 * D_MODEL), rows(D_MODEL),
                  wspec, wspec, wspec, pl.BlockSpec((1, D_MODEL), const)],
        out_specs=rows(D_MODEL),
        out_shape=jax.ShapeDtypeStruct((m, D_MODEL), F32),
        scratch_shapes=[pltpu.VMEM((D_MODEL, D_MODEL), BF16)] * 3,
        compiler_params=pltpu.CompilerParams(
            dimension_semantics=("arbitrary",), vmem_limit_bytes=VMEM_LIMIT),
        name="outproj",
    )(oa, ob, sg, gates, x2, wpa, wpb, wo, final_g)


def kernel(x, norm_g, w_in, w_dec_up, b_dec, gla_norm_g, w_pa, w_pb, b_gate, w_o, final_g):
    batch, seq, _ = x.shape
    x2 = x.reshape(batch * seq, D_MODEL)

    wdu = jnp.pad(w_dec_up, ((0, RANK_PAD - GLA_RANK), (0, 0))).astype(BF16)

    oa, ob, sg, gates = _mixers(
        x2, norm_g.reshape(1, D_MODEL), _pack_w_in(w_in), b_gate.reshape(1, 2 * D_MODEL),
        wdu, b_dec.reshape(1, GLA_DK), gla_norm_g.reshape(1, GLA_HV), seq)

    out = _out(oa, ob, sg, gates, x2, w_pa, w_pb, w_o, final_g.reshape(1, D_MODEL), tm=512)
    return out.reshape(batch, seq, D_MODEL)
```

```python
import functools

import jax
import jax.numpy as jnp
from jax import lax
from jax.experimental import pallas as pl
from jax.experimental.pallas import tpu as pltpu

D_MODEL = 1024
GLA_HEADS = 4
GLA_HK = 128
GLA_HV = 256
GLA_DK = GLA_HEADS * GLA_HK
GLA_DV = GLA_HEADS * GLA_HV
GLA_RANK = 16
GLA_TAU = 16.0
GLA_CHUNK = 64
SB_HEADS = 8
SB_HD = 128
SB_WIDTH = SB_HEADS * SB_HD
EPS = 1e-6
LOG2E = 1.4426950408889634
MASK_BIAS = -1e30
SB_ZERO_BITS = 160.0

GLA_GROUP = 4

LANES = 128
MXU_COLS = 256
RANK_PAD = MXU_COLS
PACK_COLS = 1024
OUT_ROWS = 256
VMEM_LIMIT = 56 * 1024 * 1024

F32 = jnp.float32
BF16 = jnp.bfloat16

_NT = (((1,), (1,)), ((), ()))
_TN = (((0,), (0,)), ((), ()))


def _sigmoid(x):
    return 1.0 / (1.0 + jnp.exp(-x))


def _softplus(x):
    return jnp.maximum(x, 0.0) + jnp.log(1.0 + jnp.exp(-jnp.abs(x)))


def _sb_block(q, k, v, bias, carry, tri):
    z = lax.dot_general(q, k, _NT, preferred_element_type=F32)
    if bias is not None:
        z = z + bias
    sp = jnp.maximum(z, 0.0) + jnp.log(1.0 + jnp.exp2(-jnp.abs(z))) * LOG2E
    x = (z - sp) - jnp.dot(sp.astype(BF16), tri, preferred_element_type=F32)
    if carry is not None:
        x = x - jnp.concatenate([carry, carry], axis=1)
    pv = jnp.dot(jnp.exp2(x).astype(BF16), v, preferred_element_type=F32)
    return pv, jnp.sum(sp, axis=1, keepdims=True)


def _mixer_kernel(x_ref, ng_ref, wgq, wgk, wgv, wgg, wgr, wsq, wsk, wsv, wsg, wm, bg_ref,
                  wdu_ref, bdec_ref, gng_ref,
                  oa_o, ob_o, sg_o, m_o,
                  st_ref, q_scr, k_scr, v_scr, acc_ref, run_ref, *, tiles_per_seq):
    c = GLA_CHUNK
    t = GLA_GROUP * c
    i = pl.program_id(0)
    last = pl.num_programs(0) - 1
    p = jnp.maximum(i - 1, 0)

    @pl.when(i == 0)
    def _():
        q_scr[...] = jnp.zeros((t, SB_WIDTH), BF16)
        k_scr[pl.ds(0, t), :] = jnp.zeros((t, SB_WIDTH), BF16)
        v_scr[pl.ds(0, t), :] = jnp.zeros((t, SB_WIDTH), BF16)

    @pl.when(i % tiles_per_seq == 0)
    def _():
        st_ref[...] = jnp.zeros_like(st_ref)

    row = lax.broadcasted_iota(jnp.int32, (t, t), 0)
    col = lax.broadcasted_iota(jnp.int32, (t, t), 1)
    tri = jnp.where(row > col, 1.0, 0.0).astype(BF16)
    causal_bias = jnp.where(col < row, 0.0, MASK_BIAS)
    bd_tril = (row >= col) & ((row ^ col) < c)
    bd_tril_b = jnp.where(bd_tril, 1.0, 0.0).astype(BF16)

    def sb_head(hd, tile):
        qt = tile % tiles_per_seq
        rows_d = pl.ds(pl.multiple_of(qt * t, t), t)
        rows_p = pl.ds(pl.multiple_of(jnp.maximum(qt - 1, 0) * t, t), t)
        has_prev = (qt > 0).astype(F32)
        hs = slice(hd * SB_HD, (hd + 1) * SB_HD)
        qh = q_scr[:, hs]
        pv_d, rs_d = _sb_block(qh, k_scr[rows_d, hs], v_scr[rows_d, hs], causal_bias, None, tri)
        carry = jnp.broadcast_to(rs_d, (t, LANES))
        pv_p, rs_p = _sb_block(qh, k_scr[rows_p, hs], v_scr[rows_p, hs], None, carry, tri)
        acc_ref[:, hs] = pv_d + pv_p * has_prev
        run = carry + rs_p * has_prev
        run_ref[hd] = run
        return run

    def sb_rest(tile, lowest):
        qt = tile % tiles_per_seq

        def more(state):
            kb, low = state
            return (kb >= 0) & (low <= SB_ZERO_BITS)

        def walk(state):
            kb, _ = state
            keys = pl.ds(pl.multiple_of(kb * t, t), t)
            tri_w = jnp.where(row > col, 1.0, 0.0).astype(BF16)
            low = None
            for hd in range(SB_HEADS):
                hs = slice(hd * SB_HD, (hd + 1) * SB_HD)
                carry = run_ref[hd]
                pv, rs = _sb_block(q_scr[:, hs], k_scr[keys, hs], v_scr[keys, hs],
                                   None, carry, tri_w)
                acc_ref[:, hs] += pv
                run = carry + rs
                run_ref[hd] = run
                low = run if low is None else jnp.minimum(low, run)
            return kb - 1, jnp.min(low)

        lax.while_loop(more, walk, (qt - 2, lowest))
        ob_o[pl.ds(pl.multiple_of((tile % 2) * t, t), t), :] = acc_ref[...].astype(BF16)

    def gla_head(gh, qe, ke, kd, v, gate, lasts):
        ks = slice(gh * GLA_HK, (gh + 1) * GLA_HK)
        vs = slice(gh * GLA_HV, (gh + 1) * GLA_HV)
        attn = lax.dot_general(qe[:, ks], ke[:, ks], _NT, preferred_element_type=F32)
        attn = jnp.where(bd_tril, attn, 0.0).astype(BF16)
        intra = jnp.dot(attn, v[:, vs], preferred_element_type=F32)
        st = st_ref[gh]
        for j in range(GLA_GROUP):
            rows = slice(j * c, (j + 1) * c)
            o = intra[rows] + lax.dot_general(
                qe[rows, ks], st.astype(BF16), _NT, preferred_element_type=F32)
            st = st * jnp.exp(lasts[j][:, ks]) + lax.dot_general(
                v[rows, vs], kd[rows, ks], _TN, preferred_element_type=F32)
            on = o * lax.rsqrt(jnp.mean(o * o, axis=-1, keepdims=True) + EPS) * gng_ref[...]
            oa_o[rows, vs] = (on * gate[rows, vs]).astype(BF16)
        st_ref[gh] = st

    x = x_ref[...]
    h = x * lax.rsqrt(jnp.mean(x * x, axis=-1, keepdims=True) + EPS) * ng_ref[...]
    hb = h.astype(BF16)

    def proj(w_ref):
        return jnp.dot(hb, w_ref[...], preferred_element_type=F32)

    cur = {}

    half = SB_WIDTH // 2

    def piece_sq(j):
        cols = slice(j * half, (j + 1) * half)
        def run_piece():
            cur["sq", j] = (proj(wsq.at[:, cols]) * (SB_HD ** -0.5 * LOG2E)).astype(BF16)
        return run_piece

    def piece_skv(name, w_ref, j):
        cols = slice(j * half, (j + 1) * half)
        def run_piece():
            cur[name, j] = proj(w_ref.at[:, cols]).astype(BF16)
        return run_piece

    def piece_gla_in():
        cur["code"] = proj(wgr).astype(BF16)
        cur["q"] = proj(wgq) * (GLA_HK ** -0.5)
        cur["k"] = proj(wgk)

    def piece_decay():
        u = jnp.dot(cur["code"], wdu_ref[...], preferred_element_type=F32) + bdec_ref[...]
        la = -_softplus(-u) * (1.0 / GLA_TAU)
        la_hi = la.astype(BF16)
        la_lo = (la - la_hi.astype(F32)).astype(BF16)
        cur["bcum"] = (jnp.dot(bd_tril_b, la_hi, preferred_element_type=F32)
                       + jnp.dot(bd_tril_b, la_lo, preferred_element_type=F32))

    def piece_qkd():
        bcum = cur["bcum"]
        cur["v"] = proj(wgv).astype(BF16)
        cur["lasts"] = [bcum[(j + 1) * c - 1:(j + 1) * c, :] for j in range(GLA_GROUP)]
        blast = jnp.concatenate([jnp.broadcast_to(l, (c, GLA_DK)) for l in cur["lasts"]], axis=0)
        cur["qe"] = (cur["q"] * jnp.exp(bcum)).astype(BF16)
        cur["ke"] = (cur["k"] * jnp.exp(-bcum)).astype(BF16)
        cur["kd"] = (cur["k"] * jnp.exp(blast - bcum)).astype(BF16)

    def piece_gate():
        g = proj(wgg)
        cur["gate"] = g * _sigmoid(g)

    def piece_gla(gh):
        return lambda: gla_head(gh, cur["qe"], cur["ke"], cur["kd"], cur["v"], cur["gate"],
                                cur["lasts"])

    def piece_sg(j):
        cols = slice(j * half, (j + 1) * half)
        def run_piece():
            g = proj(wsg.at[:, cols])
            sg_o[:, cols] = (g * _sigmoid(g)).astype(BF16)
        return run_piece

    def piece_m(j):
        cols = slice(j * half, (j + 1) * half)
        def run_piece():
            m_o[:, cols] = _sigmoid(proj(wm.at[:, cols]) + bg_ref[:, cols]).astype(BF16)
        return run_piece

    pieces = [piece_gla_in, piece_decay, piece_qkd, piece_gate,
              piece_gla(0), piece_sq(0), piece_gla(1), piece_sq(1),
              piece_gla(2), piece_skv("sk", wsk, 0), piece_gla(3), piece_skv("sk", wsk, 1),
              piece_skv("sv", wsv, 0), piece_skv("sv", wsv, 1), piece_sg(0), piece_sg(1),
              lambda: (piece_m(0)(), piece_m(1)()), lambda: (piece_m(2)(), piece_m(3)())]

    qt_p = p % tiles_per_seq
    rows_d = pl.ds(pl.multiple_of(qt_p * t, t), t)
    rows_b = pl.ds(pl.multiple_of(jnp.maximum(qt_p - 1, 0) * t, t), t)
    has_prev = (qt_p > 0).astype(F32)
    n_blocks = 2 * SB_HEADS
    blk = [dict() for _ in range(n_blocks)]

    def stage1(n):
        hd, before = divmod(n, 2)
        hs = slice(hd * SB_HD, (hd + 1) * SB_HD)
        rows = rows_b if before else rows_d
        z = lax.dot_general(q_scr[:, hs], k_scr[rows, hs], _NT, preferred_element_type=F32)
        if not before:
            z = z + causal_bias
        sp = jnp.maximum(z, 0.0) + jnp.log(1.0 + jnp.exp2(-jnp.abs(z))) * LOG2E
        blk[n].update(sp=sp.astype(BF16), lb=z - sp, rs=jnp.sum(sp, axis=1, keepdims=True))

    def stage2(n):
        x = blk[n]["lb"] - jnp.dot(blk[n]["sp"], tri, preferred_element_type=F32)
        if n % 2:
            carry = jnp.broadcast_to(blk[n - 1]["rs"], (t, LANES))
            x = x - jnp.concatenate([carry, carry], axis=1)
        blk[n]["a"] = jnp.exp2(x).astype(BF16)

    def stage3(n):
        hd, before = divmod(n, 2)
        hs = slice(hd * SB_HD, (hd + 1) * SB_HD)
        rows = rows_b if before else rows_d
        blk[n]["pv"] = jnp.dot(blk[n]["a"], v_scr[rows, hs], preferred_element_type=F32)
        if before:
            acc_ref[:, hs] = blk[n - 1]["pv"] + blk[n]["pv"] * has_prev
            run = jnp.broadcast_to(blk[n - 1]["rs"] + blk[n]["rs"] * has_prev, (t, LANES))
            run_ref[hd] = run
            blk[n]["run"] = run

    for n in range(n_blocks + 2):
        if 0 <= n - 2 < n_blocks:
            stage3(n - 2)
        if 0 <= n - 1 < n_blocks:
            stage2(n - 1)
        if n < n_blocks:
            stage1(n)
        if n < len(pieces):
            pieces[n]()
    run_min = blk[1]["run"]
    for hd in range(1, SB_HEADS):
        run_min = jnp.minimum(run_min, blk[2 * hd + 1]["run"])

    sb_rest(p, jnp.min(run_min))

    rows_now = pl.ds(pl.multiple_of((i % tiles_per_seq) * t, t), t)
    for j in range(2):
        cols = slice(j * half, (j + 1) * half)
        q_scr[:, cols] = cur["sq", j]
        k_scr[rows_now, cols] = cur["sk", j]
        v_scr[rows_now, cols] = cur["sv", j]

    @pl.when(i == last)
    def _():
        low = sb_head(0, i)
        for hd in range(1, SB_HEADS):
            low = jnp.minimum(low, sb_head(hd, i))
        sb_rest(i, jnp.min(low))


_REF_HEAD = 2 * GLA_DK + 2 * GLA_DV
_REF_TAIL = _REF_HEAD + GLA_RANK
_REF_M = _REF_TAIL + 4 * SB_WIDTH
_REF_COLS = _REF_M + 2 * D_MODEL
_PK_HEAD = 2 * D_MODEL
_PK_TAIL = _PK_HEAD + _REF_HEAD
_PK_RANK = _PK_TAIL + 4 * SB_WIDTH
_PK_COLS = _PK_RANK + RANK_PAD
_W_BLOCKS = ((GLA_DK, _PK_HEAD // GLA_DK), (GLA_DK, _PK_HEAD // GLA_DK + 1),
             (GLA_DV, (_PK_HEAD + 2 * GLA_DK) // GLA_DV), (GLA_DV, (_PK_HEAD + 2 * GLA_DK) // GLA_DV + 1),
             (RANK_PAD, _PK_RANK // RANK_PAD),
             (SB_WIDTH, _PK_TAIL // SB_WIDTH), (SB_WIDTH, _PK_TAIL // SB_WIDTH + 1),
             (SB_WIDTH, _PK_TAIL // SB_WIDTH + 2), (SB_WIDTH, _PK_TAIL // SB_WIDTH + 3),
             (2 * D_MODEL, 0))


def _pack_kernel(src_ref, wt_ref, o_ref):
    j = pl.program_id(0)
    w = wt_ref[...]
    n_row = lax.broadcasted_iota(jnp.int32, w.shape, 0)
    is_code = j == pl.num_programs(0) - 1
    w = jnp.where(is_code & (n_row >= GLA_RANK), 0.0, w).astype(BF16)
    r = lax.broadcasted_iota(jnp.int32, (MXU_COLS, MXU_COLS), 0)
    cidx = lax.broadcasted_iota(jnp.int32, (MXU_COLS, MXU_COLS), 1)
    eye = jnp.where(r == cidx, 1.0, 0.0).astype(BF16)
    for nn in range(0, w.shape[0], MXU_COLS):
        for kk in range(0, w.shape[1], MXU_COLS):
            o_ref[kk:kk + MXU_COLS, nn:nn + MXU_COLS] = lax.dot_general(
                eye, w[nn:nn + MXU_COLS, kk:kk + MXU_COLS], _NT,
                preferred_element_type=F32).astype(BF16)


def _pack_w_in(w_in):
    d = w_in.shape[0]
    assert w_in.shape[1] == _REF_COLS
    src = ([_REF_M + c for c in range(0, _PK_HEAD, PACK_COLS)]
           + [c for c in range(0, _REF_HEAD, PACK_COLS)]
           + [_REF_TAIL + c for c in range(0, _REF_M - _REF_TAIL, PACK_COLS)]
           + [_REF_HEAD])
    assert (len(src) - 1) * PACK_COLS == _PK_RANK and _REF_HEAD + PACK_COLS <= _REF_COLS
    assert all(c % GLA_RANK == 0 for c in src)
    src = [c // GLA_RANK for c in src]
    grid_spec = pltpu.PrefetchScalarGridSpec(
        num_scalar_prefetch=1,
        grid=(len(src),),
        in_specs=[pl.BlockSpec((pl.Element(PACK_COLS), pl.Element(d)),
                               lambda j, src_ref: (src_ref[j] * GLA_RANK, 0))],
        out_specs=pl.BlockSpec((d, PACK_COLS), lambda j, src_ref: (0, j)))
    return pl.pallas_call(
        _pack_kernel,
        grid_spec=grid_spec,
        out_shape=jax.ShapeDtypeStruct((d, _PK_COLS), BF16),
        compiler_params=pltpu.CompilerParams(
            dimension_semantics=("parallel",), vmem_limit_bytes=VMEM_LIMIT),
        name="packw",
    )(jnp.asarray(src, jnp.int32), w_in.T)


def _mixers(x2, norm_g, w_all, b_gate, wdu, b_dec, gla_norm_g, seq):
    m = x2.shape[0]
    tm = GLA_GROUP * GLA_CHUNK
    const = lambda i: (0, 0)
    w_specs = [pl.BlockSpec((D_MODEL, n), functools.partial(lambda i, c: (0, c), c=c),
                            pipeline_mode=pl.Buffered(1))
               for n, c in _W_BLOCKS]
    in_specs = ([pl.BlockSpec((tm, D_MODEL), lambda i: (i, 0)),
                 pl.BlockSpec((1, D_MODEL), const)]
                + w_specs
                + [pl.BlockSpec((1, 2 * D_MODEL), const),
                   pl.BlockSpec((RANK_PAD, GLA_DK), const),
                   pl.BlockSpec((1, GLA_DK), const),
                   pl.BlockSpec((1, GLA_HV), const)])
    out_widths = [GLA_DV, SB_WIDTH, SB_WIDTH, 2 * D_MODEL]
    out_specs = [pl.BlockSpec((tm, n), lambda i: (i, 0)) for n in out_widths]
    out_specs[1] = pl.BlockSpec((2 * tm, SB_WIDTH), lambda i: (jnp.maximum(i - 1, 0) // 2, 0))
    out_shape = [jax.ShapeDtypeStruct((m, n), BF16) for n in out_widths]
    return pl.pallas_call(
        functools.partial(_mixer_kernel, tiles_per_seq=seq // tm),
        grid=(m // tm,),
        in_specs=in_specs,
        out_specs=out_specs,
        out_shape=out_shape,
        scratch_shapes=[
            pltpu.VMEM((GLA_HEADS, GLA_HV, GLA_HK), F32),
            pltpu.VMEM((tm, SB_WIDTH), BF16),
            pltpu.VMEM((seq, SB_WIDTH), BF16),
            pltpu.VMEM((seq, SB_WIDTH), BF16),
            pltpu.VMEM((tm, SB_WIDTH), F32),
            pltpu.VMEM((SB_HEADS, tm, LANES), F32),
        ],
        compiler_params=pltpu.CompilerParams(
            dimension_semantics=("arbitrary",), vmem_limit_bytes=VMEM_LIMIT),
        name="mixers",
    )(x2, norm_g, *([w_all] * len(_W_BLOCKS)), b_gate, wdu, b_dec, gla_norm_g)


def _out_kernel(oa_ref, ob_ref, sg_ref, m_ref, x_ref, wpa_ref, wpb_ref, wo_ref, fg_ref, o_ref,
                wpa_b, wpb_b, wo_b):
    @pl.when(pl.program_id(0) == 0)
    def _():
        wpa_b[...] = wpa_ref[...].astype(BF16)
        wpb_b[...] = wpb_ref[...].astype(BF16)
        wo_b[...] = wo_ref[...].astype(BF16)

    tm = o_ref.shape[0]
    groups = [pl.ds(r, OUT_ROWS) for r in range(0, tm, OUT_ROWS)]

    def branches(rows):
        ya = jnp.dot(oa_ref[rows, :], wpa_b[...], preferred_element_type=F32)
        yb = jnp.dot(ob_ref[rows, :] * sg_ref[rows, :], wpb_b[...], preferred_element_type=F32)
        return ya, yb

    def merge(rows, ya, yb):
        merged = (m_ref[rows, :D_MODEL].astype(F32) * ya + m_ref[rows, D_MODEL:].astype(F32) * yb)
        return x_ref[rows, :] + jnp.dot(merged.astype(BF16), wo_b[...],
                                        preferred_element_type=F32)

    def finish(rows, y):
        o_ref[rows, :] = y * lax.rsqrt(jnp.mean(y * y, axis=-1, keepdims=True) + EPS) * fg_ref[...]

    yab = branches(groups[0])
    y_prev = None
    for g, rows in enumerate(groups):
        yab_next = branches(groups[g + 1]) if g + 1 < len(groups) else None
        y = merge(rows, *yab)
        if y_prev is not None:
            finish(groups[g - 1], y_prev)
        yab, y_prev = yab_next, y
    finish(groups[-1], y_prev)


def _out(oa, ob, sg, gates, x2, wpa, wpb, wo, final_g, tm):
    m = x2.shape[0]
    const = lambda i: (0, 0)
    rows = lambda n: pl.BlockSpec((tm, n), lambda i: (i, 0))
    wspec = pl.BlockSpec((D_MODEL, D_MODEL), const, pipeline_mode=pl.Buffered(1))
    return pl.pallas_call(
        _out_kernel,
        grid=(m // tm,),
        in_specs=[rows(GLA_DV), rows(SB_WIDTH), rows(SB_WIDTH), rows(2 * D_MODEL), rows(D_MODEL),
                  wspec, wspec, wspec, pl.BlockSpec((1, D_MODEL), const)],
        out_specs=rows(D_MODEL),
        out_shape=jax.ShapeDtypeStruct((m, D_MODEL), F32),
        scratch_shapes=[pltpu.VMEM((D_MODEL, D_MODEL), BF16)] * 3,
        compiler_params=pltpu.CompilerParams(
            dimension_semantics=("arbitrary",), vmem_limit_bytes=VMEM_LIMIT),
        name="outproj",
    )(oa, ob, sg, gates, x2, wpa, wpb, wo, final_g)


def kernel(x, norm_g, w_in, w_dec_up, b_dec, gla_norm_g, w_pa, w_pb, b_gate, w_o, final_g):
    batch, seq, _ = x.shape
    x2 = x.reshape(batch * seq, D_MODEL)

    wdu = jnp.pad(w_dec_up, ((0, RANK_PAD - GLA_RANK), (0, 0))).astype(BF16)

    oa, ob, sg, gates = _mixers(
        x2, norm_g.reshape(1, D_MODEL), _pack_w_in(w_in), b_gate.reshape(1, 2 * D_MODEL),
        wdu, b_dec.reshape(1, GLA_DK), gla_norm_g.reshape(1, GLA_HV), seq)

    out = _out(oa, ob, sg, gates, x2, w_pa, w_pb, w_o, final_g.reshape(1, D_MODEL), tm=512)
    return out.reshape(batch, seq, D_MODEL)
```

```python
import functools

import jax
import jax.numpy as jnp
from jax import lax
from jax.experimental import pallas as pl
from jax.experimental.pallas import tpu as pltpu

D_MODEL = 1024
GLA_HEADS = 4
GLA_HK = 128
GLA_HV = 256
GLA_DK = GLA_HEADS * GLA_HK
GLA_DV = GLA_HEADS * GLA_HV
GLA_RANK = 16
GLA_TAU = 16.0
GLA_CHUNK = 64
SB_HEADS = 8
SB_HD = 128
SB_WIDTH = SB_HEADS * SB_HD
EPS = 1e-6
LOG2E = 1.4426950408889634
MASK_BIAS = -1e30
SB_ZERO_BITS = 160.0

GLA_GROUP = 4

SUBLANES = 8
MXU_COLS = 256
RANK_PAD = MXU_COLS
PACK_COLS = 1024
OUT_ROWS = 256
VMEM_LIMIT = 56 * 1024 * 1024

F32 = jnp.float32
BF16 = jnp.bfloat16

_NT = (((1,), (1,)), ((), ()))
_TN = (((0,), (0,)), ((), ()))


def _sigmoid(x):
    return 1.0 / (1.0 + jnp.exp(-x))


def _softplus(x):
    return jnp.maximum(x, 0.0) + jnp.log(1.0 + jnp.exp(-jnp.abs(x)))


def _sb_block(q, k, v, bias, carry, tri_u):
    z = lax.dot_general(k, q, _NT, preferred_element_type=F32)
    if bias is not None:
        z = z + bias
    sp = jnp.maximum(z, 0.0) + jnp.log(1.0 + jnp.exp2(-jnp.abs(z))) * LOG2E
    x = (z - sp) - jnp.dot(tri_u, sp.astype(BF16), preferred_element_type=F32)
    if carry is not None:
        x = x - carry
    pv = lax.dot_general(v, jnp.exp2(x).astype(BF16), _TN, preferred_element_type=F32)
    return pv, jnp.sum(sp, axis=0, keepdims=True)


def _mixer_kernel(x_ref, ng_ref, wgq, wgk, wgv, wgg, wgr, wsq, wsk, wsv, wsg, wm, bg_ref,
                  wdu_ref, bdec_ref, gng_ref,
                  oa_o, ob_o, sg_o, m_o,
                  st_ref, q_scr, k_scr, v_scr, acc_ref, run_ref, *, tiles_per_seq):
    c = GLA_CHUNK
    t = GLA_GROUP * c
    i = pl.program_id(0)
    last = pl.num_programs(0) - 1
    p = jnp.maximum(i - 1, 0)

    @pl.when(i == 0)
    def _():
        q_scr[...] = jnp.zeros((t, SB_WIDTH), BF16)
        k_scr[pl.ds(0, t), :] = jnp.zeros((t, SB_WIDTH), BF16)
        v_scr[pl.ds(0, t), :] = jnp.zeros((t, SB_WIDTH), BF16)

    @pl.when(i % tiles_per_seq == 0)
    def _():
        st_ref[...] = jnp.zeros_like(st_ref)

    row = lax.broadcasted_iota(jnp.int32, (t, t), 0)
    col = lax.broadcasted_iota(jnp.int32, (t, t), 1)
    tri_u = jnp.where(col > row, 1.0, 0.0).astype(BF16)
    causal_bias = jnp.where(row < col, 0.0, MASK_BIAS)
    bd_tril = (row >= col) & ((row ^ col) < c)
    bd_tril_b = jnp.where(bd_tril, 1.0, 0.0).astype(BF16)

    def sb_head(hd, tile):
        qt = tile % tiles_per_seq
        rows_d = pl.ds(pl.multiple_of(qt * t, t), t)
        rows_p = pl.ds(pl.multiple_of(jnp.maximum(qt - 1, 0) * t, t), t)
        has_prev = (qt > 0).astype(F32)
        hs = slice(hd * SB_HD, (hd + 1) * SB_HD)
        qh = q_scr[:, hs]
        pv_d, rs_d = _sb_block(qh, k_scr[rows_d, hs], v_scr[rows_d, hs], causal_bias, None, tri_u)
        pv_p, rs_p = _sb_block(qh, k_scr[rows_p, hs], v_scr[rows_p, hs], None, rs_d, tri_u)
        acc_ref[:, hs] = jnp.transpose(pv_d + pv_p * has_prev)
        run = rs_d + rs_p * has_prev
        run_ref[hd] = jnp.broadcast_to(run, (SUBLANES, t))
        return run

    def sb_rest(tile, lowest):
        qt = tile % tiles_per_seq

        def more(state):
            kb, low = state
            return (kb >= 0) & (low <= SB_ZERO_BITS)

        def walk(state):
            kb, _ = state
            keys = pl.ds(pl.multiple_of(kb * t, t), t)
            tri_w = jnp.where(col > row, 1.0, 0.0).astype(BF16)
            low = None
            for hd in range(SB_HEADS):
                hs = slice(hd * SB_HD, (hd + 1) * SB_HD)
                carry = run_ref[hd][0:1, :]
                pv, rs = _sb_block(q_scr[:, hs], k_scr[keys, hs], v_scr[keys, hs],
                                   None, carry, tri_w)
                acc_ref[:, hs] += jnp.transpose(pv)
                run = carry + rs
                run_ref[hd] = jnp.broadcast_to(run, (SUBLANES, t))
                low = run if low is None else jnp.minimum(low, run)
            return kb - 1, jnp.min(low)

        lax.while_loop(more, walk, (qt - 2, lowest))
        ob_o[pl.ds(pl.multiple_of((tile % 2) * t, t), t), :] = acc_ref[...].astype(BF16)

    def gla_head(gh, qe, ke, kd, v, gate, lasts):
        ks = slice(gh * GLA_HK, (gh + 1) * GLA_HK)
        vs = slice(gh * GLA_HV, (gh + 1) * GLA_HV)
        attn = lax.dot_general(qe[:, ks], ke[:, ks], _NT, preferred_element_type=F32)
        attn = jnp.where(bd_tril, attn, 0.0).astype(BF16)
        intra = jnp.dot(attn, v[:, vs], preferred_element_type=F32)
        st = st_ref[gh]
        for j in range(GLA_GROUP):
            rows = slice(j * c, (j + 1) * c)
            o = intra[rows] + lax.dot_general(
                qe[rows, ks], st.astype(BF16), _NT, preferred_element_type=F32)
            st = st * jnp.exp(lasts[j][:, ks]) + lax.dot_general(
                v[rows, vs], kd[rows, ks], _TN, preferred_element_type=F32)
            on = o * lax.rsqrt(jnp.mean(o * o, axis=-1, keepdims=True) + EPS) * gng_ref[...]
            oa_o[rows, vs] = (on * gate[rows, vs]).astype(BF16)
        st_ref[gh] = st

    x = x_ref[...]
    h = x * lax.rsqrt(jnp.mean(x * x, axis=-1, keepdims=True) + EPS) * ng_ref[...]
    hb = h.astype(BF16)

    def proj(w_ref):
        return jnp.dot(hb, w_ref[...], preferred_element_type=F32)

    cur = {}

    half = SB_WIDTH // 2

    def piece_sq(j):
        cols = slice(j * half, (j + 1) * half)
        def run_piece():
            cur["sq", j] = (proj(wsq.at[:, cols]) * (SB_HD ** -0.5 * LOG2E)).astype(BF16)
        return run_piece

    def piece_skv(name, w_ref, j):
        cols = slice(j * half, (j + 1) * half)
        def run_piece():
            cur[name, j] = proj(w_ref.at[:, cols]).astype(BF16)
        return run_piece

    def piece_gla_in():
        cur["code"] = proj(wgr).astype(BF16)
        cur["q"] = proj(wgq) * (GLA_HK ** -0.5)
        cur["k"] = proj(wgk)

    def piece_decay():
        u = jnp.dot(cur["code"], wdu_ref[...], preferred_element_type=F32) + bdec_ref[...]
        la = -_softplus(-u) * (1.0 / GLA_TAU)
        la_hi = la.astype(BF16)
        la_lo = (la - la_hi.astype(F32)).astype(BF16)
        cur["bcum"] = (jnp.dot(bd_tril_b, la_hi, preferred_element_type=F32)
                       + jnp.dot(bd_tril_b, la_lo, preferred_element_type=F32))

    def piece_qkd():
        bcum = cur["bcum"]
        cur["v"] = proj(wgv).astype(BF16)
        cur["lasts"] = [bcum[(j + 1) * c - 1:(j + 1) * c, :] for j in range(GLA_GROUP)]
        blast = jnp.concatenate([jnp.broadcast_to(l, (c, GLA_DK)) for l in cur["lasts"]], axis=0)
        cur["qe"] = (cur["q"] * jnp.exp(bcum)).astype(BF16)
        cur["ke"] = (cur["k"] * jnp.exp(-bcum)).astype(BF16)
        cur["kd"] = (cur["k"] * jnp.exp(blast - bcum)).astype(BF16)

    def piece_gate():
        g = proj(wgg)
        cur["gate"] = g * _sigmoid(g)

    def piece_gla(gh):
        return lambda: gla_head(gh, cur["qe"], cur["ke"], cur["kd"], cur["v"], cur["gate"],
                                cur["lasts"])

    def piece_sg(j):
        cols = slice(j * half, (j + 1) * half)
        def run_piece():
            g = proj(wsg.at[:, cols])
            sg_o[:, cols] = (g * _sigmoid(g)).astype(BF16)
        return run_piece

    def piece_m(j):
        cols = slice(j * half, (j + 1) * half)
        def run_piece():
            m_o[:, cols] = _sigmoid(proj(wm.at[:, cols]) + bg_ref[:, cols]).astype(BF16)
        return run_piece

    pieces = [piece_gla_in, piece_decay, piece_qkd, piece_gate,
              piece_gla(0), piece_sq(0), piece_gla(1), piece_sq(1),
              piece_gla(2), piece_skv("sk", wsk, 0), piece_gla(3), piece_skv("sk", wsk, 1),
              piece_skv("sv", wsv, 0), piece_skv("sv", wsv, 1), piece_sg(0), piece_sg(1),
              lambda: (piece_m(0)(), piece_m(1)()), lambda: (piece_m(2)(), piece_m(3)())]

    qt_p = p % tiles_per_seq
    rows_d = pl.ds(pl.multiple_of(qt_p * t, t), t)
    rows_b = pl.ds(pl.multiple_of(jnp.maximum(qt_p - 1, 0) * t, t), t)
    has_prev = (qt_p > 0).astype(F32)
    n_blocks = 2 * SB_HEADS
    blk = [dict() for _ in range(n_blocks)]

    def stage1(n):
        hd, before = divmod(n, 2)
        hs = slice(hd * SB_HD, (hd + 1) * SB_HD)
        rows = rows_b if before else rows_d
        z = lax.dot_general(k_scr[rows, hs], q_scr[:, hs], _NT, preferred_element_type=F32)
        if not before:
            z = z + causal_bias
        sp = jnp.maximum(z, 0.0) + jnp.log(1.0 + jnp.exp2(-jnp.abs(z))) * LOG2E
        blk[n].update(sp=sp.astype(BF16), lb=z - sp, rs=jnp.sum(sp, axis=0, keepdims=True))

    def stage2(n):
        x = blk[n]["lb"] - jnp.dot(tri_u, blk[n]["sp"], preferred_element_type=F32)
        if n % 2:
            x = x - blk[n - 1]["rs"]
        blk[n]["a"] = jnp.exp2(x).astype(BF16)

    def stage3(n):
        hd, before = divmod(n, 2)
        hs = slice(hd * SB_HD, (hd + 1) * SB_HD)
        rows = rows_b if before else rows_d
        blk[n]["pv"] = lax.dot_general(v_scr[rows, hs], blk[n]["a"], _TN,
                                       preferred_element_type=F32)
        if before:
            acc_ref[:, hs] = jnp.transpose(blk[n - 1]["pv"] + blk[n]["pv"] * has_prev)
            run = blk[n - 1]["rs"] + blk[n]["rs"] * has_prev
            run_ref[hd] = jnp.broadcast_to(run, (SUBLANES, t))
            blk[n]["run"] = run

    for n in range(n_blocks + 2):
        if 0 <= n - 2 < n_blocks:
            stage3(n - 2)
        if 0 <= n - 1 < n_blocks:
            stage2(n - 1)
        if n < n_blocks:
            stage1(n)
        if n < len(pieces):
            pieces[n]()
    run_min = blk[1]["run"]
    for hd in range(1, SB_HEADS):
        run_min = jnp.minimum(run_min, blk[2 * hd + 1]["run"])

    sb_rest(p, jnp.min(run_min))

    rows_now = pl.ds(pl.multiple_of((i % tiles_per_seq) * t, t), t)
    for j in range(2):
        cols = slice(j * half, (j + 1) * half)
        q_scr[:, cols] = cur["sq", j]
        k_scr[rows_now, cols] = cur["sk", j]
        v_scr[rows_now, cols] = cur["sv", j]

    @pl.when(i == last)
    def _():
        low = sb_head(0, i)
        for hd in range(1, SB_HEADS):
            low = jnp.minimum(low, sb_head(hd, i))
        sb_rest(i, jnp.min(low))


_REF_HEAD = 2 * GLA_DK + 2 * GLA_DV
_REF_TAIL = _REF_HEAD + GLA_RANK
_REF_M = _REF_TAIL + 4 * SB_WIDTH
_REF_COLS = _REF_M + 2 * D_MODEL
_PK_HEAD = 2 * D_MODEL
_PK_TAIL = _PK_HEAD + _REF_HEAD
_PK_RANK = _PK_TAIL + 4 * SB_WIDTH
_PK_COLS = _PK_RANK + RANK_PAD
_W_BLOCKS = ((GLA_DK, _PK_HEAD // GLA_DK), (GLA_DK, _PK_HEAD // GLA_DK + 1),
             (GLA_DV, (_PK_HEAD + 2 * GLA_DK) // GLA_DV), (GLA_DV, (_PK_HEAD + 2 * GLA_DK) // GLA_DV + 1),
             (RANK_PAD, _PK_RANK // RANK_PAD),
             (SB_WIDTH, _PK_TAIL // SB_WIDTH), (SB_WIDTH, _PK_TAIL // SB_WIDTH + 1),
             (SB_WIDTH, _PK_TAIL // SB_WIDTH + 2), (SB_WIDTH, _PK_TAIL // SB_WIDTH + 3),
             (2 * D_MODEL, 0))


def _pack_kernel(src_ref, wt_ref, o_ref):
    j = pl.program_id(0)
    w = wt_ref[...]
    n_row = lax.broadcasted_iota(jnp.int32, w.shape, 0)
    is_code = j == pl.num_programs(0) - 1
    w = jnp.where(is_code & (n_row >= GLA_RANK), 0.0, w).astype(BF16)
    r = lax.broadcasted_iota(jnp.int32, (MXU_COLS, MXU_COLS), 0)
    cidx = lax.broadcasted_iota(jnp.int32, (MXU_COLS, MXU_COLS), 1)
    eye = jnp.where(r == cidx, 1.0, 0.0).astype(BF16)
    for nn in range(0, w.shape[0], MXU_COLS):
        for kk in range(0, w.shape[1], MXU_COLS):
            o_ref[kk:kk + MXU_COLS, nn:nn + MXU_COLS] = lax.dot_general(
                eye, w[nn:nn + MXU_COLS, kk:kk + MXU_COLS], _NT,
                preferred_element_type=F32).astype(BF16)


def _pack_w_in(w_in):
    d = w_in.shape[0]
    assert w_in.shape[1] == _REF_COLS
    src = ([_REF_M + c for c in range(0, _PK_HEAD, PACK_COLS)]
           + [c for c in range(0, _REF_HEAD, PACK_COLS)]
           + [_REF_TAIL + c for c in range(0, _REF_M - _REF_TAIL, PACK_COLS)]
           + [_REF_HEAD])
    assert (len(src) - 1) * PACK_COLS == _PK_RANK and _REF_HEAD + PACK_COLS <= _REF_COLS
    assert all(c % GLA_RANK == 0 for c in src)
    src = [c // GLA_RANK for c in src]
    grid_spec = pltpu.PrefetchScalarGridSpec(
        num_scalar_prefetch=1,
        grid=(len(src),),
        in_specs=[pl.BlockSpec((pl.Element(PACK_COLS), pl.Element(d)),
                               lambda j, src_ref: (src_ref[j] * GLA_RANK, 0))],
        out_specs=pl.BlockSpec((d, PACK_COLS), lambda j, src_ref: (0, j)))
    return pl.pallas_call(
        _pack_kernel,
        grid_spec=grid_spec,
        out_shape=jax.ShapeDtypeStruct((d, _PK_COLS), BF16),
        compiler_params=pltpu.CompilerParams(
            dimension_semantics=("parallel",), vmem_limit_bytes=VMEM_LIMIT),
        name="packw",
    )(jnp.asarray(src, jnp.int32), w_in.T)


def _mixers(x2, norm_g, w_all, b_gate, wdu, b_dec, gla_norm_g, seq):
    m = x2.shape[0]
    tm = GLA_GROUP * GLA_CHUNK
    const = lambda i: (0, 0)
    w_specs = [pl.BlockSpec((D_MODEL, n), functools.partial(lambda i, c: (0, c), c=c),
                            pipeline_mode=pl.Buffered(1))
               for n, c in _W_BLOCKS]
    in_specs = ([pl.BlockSpec((tm, D_MODEL), lambda i: (i, 0)),
                 pl.BlockSpec((1, D_MODEL), const)]
                + w_specs
                + [pl.BlockSpec((1, 2 * D_MODEL), const),
                   pl.BlockSpec((RANK_PAD, GLA_DK), const),
                   pl.BlockSpec((1, GLA_DK), const),
                   pl.BlockSpec((1, GLA_HV), const)])
    out_widths = [GLA_DV, SB_WIDTH, SB_WIDTH, 2 * D_MODEL]
    out_specs = [pl.BlockSpec((tm, n), lambda i: (i, 0)) for n in out_widths]
    out_specs[1] = pl.BlockSpec((2 * tm, SB_WIDTH), lambda i: (jnp.maximum(i - 1, 0) // 2, 0))
    out_shape = [jax.ShapeDtypeStruct((m, n), BF16) for n in out_widths]
    return pl.pallas_call(
        functools.partial(_mixer_kernel, tiles_per_seq=seq // tm),
        grid=(m // tm,),
        in_specs=in_specs,
        out_specs=out_specs,
        out_shape=out_shape,
        scratch_shapes=[
            pltpu.VMEM((GLA_HEADS, GLA_HV, GLA_HK), F32),
            pltpu.VMEM((tm, SB_WIDTH), BF16),
            pltpu.VMEM((seq, SB_WIDTH), BF16),
            pltpu.VMEM((seq, SB_WIDTH), BF16),
            pltpu.VMEM((tm, SB_WIDTH), F32),
            pltpu.VMEM((SB_HEADS, SUBLANES, tm), F32),
        ],
        compiler_params=pltpu.CompilerParams(
            dimension_semantics=("arbitrary",), vmem_limit_bytes=VMEM_LIMIT),
        name="mixers",
    )(x2, norm_g, *([w_all] * len(_W_BLOCKS)), b_gate, wdu, b_dec, gla_norm_g)


def _out_kernel(oa_ref, ob_ref, sg_ref, m_ref, x_ref, wpa_ref, wpb_ref, wo_ref, fg_ref, o_ref,
                wpa_b, wpb_b, wo_b):
    @pl.when(pl.program_id(0) == 0)
    def _():
        wpa_b[...] = wpa_ref[...].astype(BF16)
        wpb_b[...] = wpb_ref[...].astype(BF16)
        wo_b[...] = wo_ref[...].astype(BF16)

    tm = o_ref.shape[0]
    groups = [pl.ds(r, OUT_ROWS) for r in range(0, tm, OUT_ROWS)]

    def branches(rows):
        ya = jnp.dot(oa_ref[rows, :], wpa_b[...], preferred_element_type=F32)
        yb = jnp.dot(ob_ref[rows, :] * sg_ref[rows, :], wpb_b[...], preferred_element_type=F32)
        return ya, yb

    def merge(rows, ya, yb):
        merged = (m_ref[rows, :D_MODEL].astype(F32) * ya + m_ref[rows, D_MODEL:].astype(F32) * yb)
        return x_ref[rows, :] + jnp.dot(merged.astype(BF16), wo_b[...],
                                        preferred_element_type=F32)

    def finish(rows, y):
        o_ref[rows, :] = y * lax.rsqrt(jnp.mean(y * y, axis=-1, keepdims=True) + EPS) * fg_ref[...]

    yab = branches(groups[0])
    y_prev = None
    for g, rows in enumerate(groups):
        yab_next = branches(groups[g + 1]) if g + 1 < len(groups) else None
        y = merge(rows, *yab)
        if y_prev is not None:
            finish(groups[g - 1], y_prev)
        yab, y_prev = yab_next, y
    finish(groups[-1], y_prev)


def _out(oa, ob, sg, gates, x2, wpa, wpb, wo, final_g, tm):
    m = x2.shape[0]
    const = lambda i: (0, 0)
    rows = lambda n: pl.BlockSpec((tm, n), lambda i: (i, 0))
    wspec = pl.BlockSpec((D_MODEL, D_MODEL), const, pipeline_mode=pl.Buffered(1))
    return pl.pallas_call(
        _out_kernel,
        grid=(m // tm,),
        in_specs=[rows(GLA_DV), rows(SB_WIDTH), rows(SB_WIDTH), rows(2 * D_MODEL), rows(D_MODEL),
                  wspec, wspec, wspec, pl.BlockSpec((1, D_MODEL), const)],
        out_specs=rows(D_MODEL),
        out_shape=jax.ShapeDtypeStruct((m, D_MODEL), F32),
        scratch_shapes=[pltpu.VMEM((D_MODEL, D_MODEL), BF16)] * 3,
        compiler_params=pltpu.CompilerParams(
            dimension_semantics=("arbitrary",), vmem_limit_bytes=VMEM_LIMIT),
        name="outproj",
    )(oa, ob, sg, gates, x2, wpa, wpb, wo, final_g)


def kernel(x, norm_g, w_in, w_dec_up, b_dec, gla_norm_g, w_pa, w_pb, b_gate, w_o, final_g):
    batch, seq, _ = x.shape
    x2 = x.reshape(batch * seq, D_MODEL)

    wdu = jnp.pad(w_dec_up, ((0, RANK_PAD - GLA_RANK), (0, 0))).astype(BF16)

    oa, ob, sg, gates = _mixers(
        x2, norm_g.reshape(1, D_MODEL), _pack_w_in(w_in), b_gate.reshape(1, 2 * D_MODEL),
        wdu, b_dec.reshape(1, GLA_DK), gla_norm_g.reshape(1, GLA_HV), seq)

    out = _out(oa, ob, sg, gates, x2, w_pa, w_pb, w_o, final_g.reshape(1, D_MODEL), tm=512)
    return out.reshape(batch, seq, D_MODEL)
```

```python
import functools

import jax
import jax.numpy as jnp
from jax import lax
from jax.experimental import pallas as pl
from jax.experimental.pallas import tpu as pltpu

D_MODEL = 1024
GLA_HEADS = 4
GLA_HK = 128
GLA_HV = 256
GLA_DK = GLA_HEADS * GLA_HK
GLA_DV = GLA_HEADS * GLA_HV
GLA_RANK = 16
GLA_TAU = 16.0
GLA_CHUNK = 64
SB_HEADS = 8
SB_HD = 128
SB_WIDTH = SB_HEADS * SB_HD
EPS = 1e-6
LOG2E = 1.4426950408889634
MASK_BIAS = -1e30
SB_ZERO_BITS = 160.0

GLA_GROUP = 4

SUBLANES = 8
MXU_COLS = 256
RANK_PAD = MXU_COLS
PACK_COLS = 1024
OUT_ROWS = 256
VMEM_LIMIT = 56 * 1024 * 1024

F32 = jnp.float32
BF16 = jnp.bfloat16

_NT = (((1,), (1,)), ((), ()))
_TN = (((0,), (0,)), ((), ()))


def _sigmoid(x):
    return 1.0 / (1.0 + jnp.exp(-x))


def _softplus(x):
    return jnp.maximum(x, 0.0) + jnp.log(1.0 + jnp.exp(-jnp.abs(x)))


def _sb_block(q, k, v, bias, carry, tri_u):
    z = lax.dot_general(k, q, _NT, preferred_element_type=F32)
    if bias is not None:
        z = z + bias
    sp = jnp.maximum(z, 0.0) + jnp.log(1.0 + jnp.exp2(-jnp.abs(z))) * LOG2E
    x = (z - sp) - jnp.dot(tri_u, sp.astype(BF16), preferred_element_type=F32)
    if carry is not None:
        x = x - carry
    pv = lax.dot_general(v, jnp.exp2(x).astype(BF16), _TN, preferred_element_type=F32)
    return pv, jnp.sum(sp, axis=0, keepdims=True)


def _mixer_kernel(x_ref, ng_ref, wgq, wgk, wgv, wgg, wgr, wsq, wsk, wsv, wsg, wm, bg_ref,
                  wdu_ref, bdec_ref, gng_ref,
                  oa_o, ob_o, sg_o, m_o,
                  st_ref, q_scr, k_scr, v_scr, acc_ref, run_ref, *, tiles_per_seq):
    c = GLA_CHUNK
    t = GLA_GROUP * c
    i = pl.program_id(0)
    last = pl.num_programs(0) - 1
    p = jnp.maximum(i - 1, 0)

    @pl.when(i == 0)
    def _():
        q_scr[...] = jnp.zeros((t, SB_WIDTH), BF16)
        k_scr[pl.ds(0, t), :] = jnp.zeros((t, SB_WIDTH), BF16)
        v_scr[pl.ds(0, t), :] = jnp.zeros((t, SB_WIDTH), BF16)

    @pl.when(i % tiles_per_seq == 0)
    def _():
        st_ref[...] = jnp.zeros_like(st_ref)

    row = lax.broadcasted_iota(jnp.int32, (t, t), 0)
    col = lax.broadcasted_iota(jnp.int32, (t, t), 1)
    tri_u = jnp.where(col > row, 1.0, 0.0).astype(BF16)
    causal_bias = jnp.where(row < col, 0.0, MASK_BIAS)
    bd_tril = (row >= col) & ((row ^ col) < c)
    bd_tril_b = jnp.where(bd_tril, 1.0, 0.0).astype(BF16)

    def sb_head(hd, tile):
        qt = tile % tiles_per_seq
        rows_d = pl.ds(pl.multiple_of(qt * t, t), t)
        rows_p = pl.ds(pl.multiple_of(jnp.maximum(qt - 1, 0) * t, t), t)
        has_prev = (qt > 0).astype(F32)
        hs = slice(hd * SB_HD, (hd + 1) * SB_HD)
        qh = q_scr[:, hs]
        pv_d, rs_d = _sb_block(qh, k_scr[rows_d, hs], v_scr[rows_d, hs], causal_bias, None, tri_u)
        pv_p, rs_p = _sb_block(qh, k_scr[rows_p, hs], v_scr[rows_p, hs], None, rs_d, tri_u)
        acc_ref[:, hs] = jnp.transpose(pv_d + pv_p * has_prev)
        run = rs_d + rs_p * has_prev
        run_ref[hd] = jnp.broadcast_to(run, (SUBLANES, t))
        return run

    def sb_rest(tile, lowest):
        qt = tile % tiles_per_seq

        def more(state):
            kb, low = state
            return (kb >= 0) & (low <= SB_ZERO_BITS)

        def walk(state):
            kb, _ = state
            keys = pl.ds(pl.multiple_of(kb * t, t), t)
            tri_w = jnp.where(col > row, 1.0, 0.0).astype(BF16)
            low = None
            for hd in range(SB_HEADS):
                hs = slice(hd * SB_HD, (hd + 1) * SB_HD)
                carry = run_ref[hd][0:1, :]
                pv, rs = _sb_block(q_scr[:, hs], k_scr[keys, hs], v_scr[keys, hs],
                                   None, carry, tri_w)
                acc_ref[:, hs] += jnp.transpose(pv)
                run = carry + rs
                run_ref[hd] = jnp.broadcast_to(run, (SUBLANES, t))
                low = run if low is None else jnp.minimum(low, run)
            return kb - 1, jnp.min(low)

        lax.while_loop(more, walk, (qt - 2, lowest))
        ob_o[pl.ds(pl.multiple_of((tile % 2) * t, t), t), :] = acc_ref[...].astype(BF16)

    def gla_head(gh, qe, ke, kd, v, gate, lasts):
        ks = slice(gh * GLA_HK, (gh + 1) * GLA_HK)
        vs = slice(gh * GLA_HV, (gh + 1) * GLA_HV)
        attn = lax.dot_general(qe[:, ks], ke[:, ks], _NT, preferred_element_type=F32)
        attn = jnp.where(bd_tril, attn, 0.0).astype(BF16)
        intra = jnp.dot(attn, v[:, vs], preferred_element_type=F32)
        st = st_ref[gh]
        for j in range(GLA_GROUP):
            rows = slice(j * c, (j + 1) * c)
            o = intra[rows] + lax.dot_general(
                qe[rows, ks], st.astype(BF16), _NT, preferred_element_type=F32)
            st = st * jnp.exp(lasts[j][:, ks]) + lax.dot_general(
                v[rows, vs], kd[rows, ks], _TN, preferred_element_type=F32)
            on = o * lax.rsqrt(jnp.mean(o * o, axis=-1, keepdims=True) + EPS) * gng_ref[...]
            oa_o[rows, vs] = (on * gate[rows, vs]).astype(BF16)
        st_ref[gh] = st

    x = x_ref[...]
    h = x * lax.rsqrt(jnp.mean(x * x, axis=-1, keepdims=True) + EPS) * ng_ref[...]
    hb = h.astype(BF16)

    def proj(w_ref):
        return jnp.dot(hb, w_ref[...], preferred_element_type=F32)

    cur = {}

    half = SB_WIDTH // 2

    def piece_sq(j):
        cols = slice(j * half, (j + 1) * half)
        def run_piece():
            cur["sq", j] = (proj(wsq.at[:, cols]) * (SB_HD ** -0.5 * LOG2E)).astype(BF16)
        return run_piece

    def piece_skv(name, w_ref, j):
        cols = slice(j * half, (j + 1) * half)
        def run_piece():
            cur[name, j] = proj(w_ref.at[:, cols]).astype(BF16)
        return run_piece

    def piece_gla_in():
        cur["code"] = proj(wgr).astype(BF16)
        cur["q"] = proj(wgq) * (GLA_HK ** -0.5)
        cur["k"] = proj(wgk)

    def piece_decay():
        u = jnp.dot(cur["code"], wdu_ref[...], preferred_element_type=F32) + bdec_ref[...]
        la = -_softplus(-u) * (1.0 / GLA_TAU)
        la_hi = la.astype(BF16)
        la_lo = (la - la_hi.astype(F32)).astype(BF16)
        cur["bcum"] = (jnp.dot(bd_tril_b, la_hi, preferred_element_type=F32)
                       + jnp.dot(bd_tril_b, la_lo, preferred_element_type=F32))

    def piece_qkd():
        bcum = cur["bcum"]
        cur["v"] = proj(wgv).astype(BF16)
        cur["lasts"] = [bcum[(j + 1) * c - 1:(j + 1) * c, :] for j in range(GLA_GROUP)]
        blast = jnp.concatenate([jnp.broadcast_to(l, (c, GLA_DK)) for l in cur["lasts"]], axis=0)
        cur["qe"] = (cur["q"] * jnp.exp(bcum)).astype(BF16)
        cur["ke"] = (cur["k"] * jnp.exp(-bcum)).astype(BF16)
        cur["kd"] = (cur["k"] * jnp.exp(blast - bcum)).astype(BF16)

    def piece_gate():
        g = proj(wgg)
        cur["gate"] = g * _sigmoid(g)

    def piece_gla(gh):
        return lambda: gla_head(gh, cur["qe"], cur["ke"], cur["kd"], cur["v"], cur["gate"],
                                cur["lasts"])

    def piece_sg(j):
        cols = slice(j * half, (j + 1) * half)
        def run_piece():
            g = proj(wsg.at[:, cols])
            sg_o[:, cols] = (g * _sigmoid(g)).astype(BF16)
        return run_piece

    def piece_m(j):
        cols = slice(j * half, (j + 1) * half)
        def run_piece():
            m_o[:, cols] = _sigmoid(proj(wm.at[:, cols]) + bg_ref[:, cols]).astype(BF16)
        return run_piece

    pieces = [piece_gla_in, piece_decay, piece_qkd, piece_gate,
              piece_gla(0), piece_sq(0), piece_gla(1), piece_sq(1),
              piece_gla(2), piece_skv("sk", wsk, 0), piece_gla(3), piece_skv("sk", wsk, 1),
              piece_skv("sv", wsv, 0), piece_skv("sv", wsv, 1), piece_sg(0), piece_sg(1),
              lambda: (piece_m(0)(), piece_m(1)()), lambda: (piece_m(2)(), piece_m(3)())]

    qt_p = p % tiles_per_seq
    rows_d = pl.ds(pl.multiple_of(qt_p * t, t), t)
    rows_b = pl.ds(pl.multiple_of(jnp.maximum(qt_p - 1, 0) * t, t), t)
    has_prev = (qt_p > 0).astype(F32)
    n_blocks = 2 * SB_HEADS
    blk = [dict() for _ in range(n_blocks)]

    def stage1(n):
        hd, before = divmod(n, 2)
        hs = slice(hd * SB_HD, (hd + 1) * SB_HD)
        rows = rows_b if before else rows_d
        z = lax.dot_general(k_scr[rows, hs], q_scr[:, hs], _NT, preferred_element_type=F32)
        if not before:
            z = z + causal_bias
        sp = jnp.maximum(z, 0.0) + jnp.log(1.0 + jnp.exp2(-jnp.abs(z))) * LOG2E
        blk[n].update(sp=sp.astype(BF16), lb=z - sp, rs=jnp.sum(sp, axis=0, keepdims=True))

    def stage2(n):
        x = blk[n]["lb"] - jnp.dot(tri_u, blk[n]["sp"], preferred_element_type=F32)
        if n % 2:
            x = x - blk[n - 1]["rs"]
        blk[n]["a"] = jnp.exp2(x).astype(BF16)

    def stage3(n):
        hd, before = divmod(n, 2)
        hs = slice(hd * SB_HD, (hd + 1) * SB_HD)
        rows = rows_b if before else rows_d
        blk[n]["pv"] = lax.dot_general(v_scr[rows, hs], blk[n]["a"], _TN,
                                       preferred_element_type=F32)
        if before:
            acc_ref[:, hs] = jnp.transpose(blk[n - 1]["pv"] + blk[n]["pv"] * has_prev)
            run = blk[n - 1]["rs"] + blk[n]["rs"] * has_prev
            run_ref[hd] = jnp.broadcast_to(run, (SUBLANES, t))
            blk[n]["run"] = run

    for n in range(n_blocks + 2):
        if 0 <= n - 2 < n_blocks:
            stage3(n - 2)
        if 0 <= n - 1 < n_blocks:
            stage2(n - 1)
        if n < n_blocks:
            stage1(n)
        if n < len(pieces):
            pieces[n]()
    run_min = blk[1]["run"]
    for hd in range(1, SB_HEADS):
        run_min = jnp.minimum(run_min, blk[2 * hd + 1]["run"])

    sb_rest(p, jnp.min(run_min))

    rows_now = pl.ds(pl.multiple_of((i % tiles_per_seq) * t, t), t)
    for j in range(2):
        cols = slice(j * half, (j + 1) * half)
        q_scr[:, cols] = cur["sq", j]
        k_scr[rows_now, cols] = cur["sk", j]
        v_scr[rows_now, cols] = cur["sv", j]

    @pl.when(i == last)
    def _():
        low = sb_head(0, i)
        for hd in range(1, SB_HEADS):
            low = jnp.minimum(low, sb_head(hd, i))
        sb_rest(i, jnp.min(low))


_REF_HEAD = 2 * GLA_DK + 2 * GLA_DV
_REF_TAIL = _REF_HEAD + GLA_RANK
_REF_M = _REF_TAIL + 4 * SB_WIDTH
_REF_COLS = _REF_M + 2 * D_MODEL
_PK_HEAD = 2 * D_MODEL
_PK_TAIL = _PK_HEAD + _REF_HEAD
_PK_RANK = _PK_TAIL + 4 * SB_WIDTH
_PK_COLS = _PK_RANK + RANK_PAD
_W_BLOCKS = ((GLA_DK, _PK_HEAD // GLA_DK), (GLA_DK, _PK_HEAD // GLA_DK + 1),
             (GLA_DV, (_PK_HEAD + 2 * GLA_DK) // GLA_DV), (GLA_DV, (_PK_HEAD + 2 * GLA_DK) // GLA_DV + 1),
             (RANK_PAD, _PK_RANK // RANK_PAD),
             (SB_WIDTH, _PK_TAIL // SB_WIDTH), (SB_WIDTH, _PK_TAIL // SB_WIDTH + 1),
             (SB_WIDTH, _PK_TAIL // SB_WIDTH + 2), (SB_WIDTH, _PK_TAIL // SB_WIDTH + 3),
             (2 * D_MODEL, 0))


def _pack_kernel(src_ref, wt_ref, o_ref):
    j = pl.program_id(0)
    w = wt_ref[...]
    n_row = lax.broadcasted_iota(jnp.int32, w.shape, 0)
    is_code = j == pl.num_programs(0) - 1
    w = jnp.where(is_code & (n_row >= GLA_RANK), 0.0, w).astype(BF16)
    r = lax.broadcasted_iota(jnp.int32, (MXU_COLS, MXU_COLS), 0)
    cidx = lax.broadcasted_iota(jnp.int32, (MXU_COLS, MXU_COLS), 1)
    eye = jnp.where(r == cidx, 1.0, 0.0).astype(BF16)
    for nn in range(0, w.shape[0], MXU_COLS):
        for kk in range(0, w.shape[1], MXU_COLS):
            o_ref[kk:kk + MXU_COLS, nn:nn + MXU_COLS] = lax.dot_general(
                eye, w[nn:nn + MXU_COLS, kk:kk + MXU_COLS], _NT,
                preferred_element_type=F32).astype(BF16)


def _pack_w_in(w_in):
    d = w_in.shape[0]
    assert w_in.shape[1] == _REF_COLS
    src = ([_REF_M + c for c in range(0, _PK_HEAD, PACK_COLS)]
           + [c for c in range(0, _REF_HEAD, PACK_COLS)]
           + [_REF_TAIL + c for c in range(0, _REF_M - _REF_TAIL, PACK_COLS)]
           + [_REF_HEAD])
    assert (len(src) - 1) * PACK_COLS == _PK_RANK and _REF_HEAD + PACK_COLS <= _REF_COLS
    assert all(c % GLA_RANK == 0 for c in src)
    src = [c // GLA_RANK for c in src]
    grid_spec = pltpu.PrefetchScalarGridSpec(
        num_scalar_prefetch=1,
        grid=(len(src),),
        in_specs=[pl.BlockSpec((pl.Element(PACK_COLS), pl.Element(d)),
                               lambda j, src_ref: (src_ref[j] * GLA_RANK, 0))],
        out_specs=pl.BlockSpec((d, PACK_COLS), lambda j, src_ref: (0, j)))
    return pl.pallas_call(
        _pack_kernel,
        grid_spec=grid_spec,
        out_shape=jax.ShapeDtypeStruct((d, _PK_COLS), BF16),
        compiler_params=pltpu.CompilerParams(
            dimension_semantics=("parallel",), vmem_limit_bytes=VMEM_LIMIT),
        name="packw",
    )(jnp.asarray(src, jnp.int32), w_in.T)


def _mixers(x2, norm_g, w_all, b_gate, wdu, b_dec, gla_norm_g, seq):
    m = x2.shape[0]
    tm = GLA_GROUP * GLA_CHUNK
    const = lambda i: (0, 0)
    w_specs = [pl.BlockSpec((D_MODEL, n), functools.partial(lambda i, c: (0, c), c=c),
                            pipeline_mode=pl.Buffered(1))
               for n, c in _W_BLOCKS]
    in_specs = ([pl.BlockSpec((tm, D_MODEL), lambda i: (i, 0)),
                 pl.BlockSpec((1, D_MODEL), const)]
                + w_specs
                + [pl.BlockSpec((1, 2 * D_MODEL), const),
                   pl.BlockSpec((RANK_PAD, GLA_DK), const),
                   pl.BlockSpec((1, GLA_DK), const),
                   pl.BlockSpec((1, GLA_HV), const)])
    out_widths = [GLA_DV, SB_WIDTH, SB_WIDTH, 2 * D_MODEL]
    out_specs = [pl.BlockSpec((tm, n), lambda i: (i, 0)) for n in out_widths]
    out_specs[1] = pl.BlockSpec((2 * tm, SB_WIDTH), lambda i: (jnp.maximum(i - 1, 0) // 2, 0))
    out_shape = [jax.ShapeDtypeStruct((m, n), BF16) for n in out_widths]
    return pl.pallas_call(
        functools.partial(_mixer_kernel, tiles_per_seq=seq // tm),
        grid=(m // tm,),
        in_specs=in_specs,
        out_specs=out_specs,
        out_shape=out_shape,
        scratch_shapes=[
            pltpu.VMEM((GLA_HEADS, GLA_HV, GLA_HK), F32),
            pltpu.VMEM((tm, SB_WIDTH), BF16),
            pltpu.VMEM((seq, SB_WIDTH), BF16),
            pltpu.VMEM((seq, SB_WIDTH), BF16),
            pltpu.VMEM((tm, SB_WIDTH), F32),
            pltpu.VMEM((SB_HEADS, SUBLANES, tm), F32),
        ],
        compiler_params=pltpu.CompilerParams(
            dimension_semantics=("arbitrary",), vmem_limit_bytes=VMEM_LIMIT),
        name="mixers",
    )(x2, norm_g, *([w_all] * len(_W_BLOCKS)), b_gate, wdu, b_dec, gla_norm_g)


def _out_kernel(oa_ref, ob_ref, sg_ref, m_ref, x_ref, wpa_ref, wpb_ref, wo_ref, fg_ref, o_ref,
                wpa_b, wpb_b, wo_b):
    @pl.when(pl.program_id(0) == 0)
    def _():
        wpa_b[...] = wpa_ref[...].astype(BF16)
        wpb_b[...] = wpb_ref[...].astype(BF16)
        wo_b[...] = wo_ref[...].astype(BF16)

    tm = o_ref.shape[0]
    groups = [pl.ds(r, OUT_ROWS) for r in range(0, tm, OUT_ROWS)]

    def branches(rows):
        ya = jnp.dot(oa_ref[rows, :], wpa_b[...], preferred_element_type=F32)
        yb = jnp.dot(ob_ref[rows, :] * sg_ref[rows, :], wpb_b[...], preferred_element_type=F32)
        return ya, yb

    def merge(rows, ya, yb):
        merged = (m_ref[rows, :D_MODEL].astype(F32) * ya + m_ref[rows, D_MODEL:].astype(F32) * yb)
        return x_ref[rows, :] + jnp.dot(merged.astype(BF16), wo_b[...],
                                        preferred_element_type=F32)

    def finish(rows, y):
        o_ref[rows, :] = y * lax.rsqrt(jnp.mean(y * y, axis=-1, keepdims=True) + EPS) * fg_ref[...]

    yab = branches(groups[0])
    y_prev = None
    for g, rows in enumerate(groups):
        yab_next = branches(groups[g + 1]) if g + 1 < len(groups) else None
        y = merge(rows, *yab)
        if y_prev is not None:
            finish(groups[g - 1], y_prev)
        yab, y_prev = yab_next, y
    finish(groups[-1], y_prev)


def _out(oa, ob, sg, gates, x2, wpa, wpb, wo, final_g, tm):
    m = x2.shape[0]
    const = lambda i: (0, 0)
    rows = lambda n: pl.BlockSpec((tm, n), lambda i: (i, 0))
    wspec = pl.BlockSpec((D_MODEL, D_MODEL), const, pipeline_mode=pl.Buffered(1))
    return pl.pallas_call(
        _out_kernel,
        grid=(m // tm,),
        in_specs=[rows(GLA_DV), rows(SB_WIDTH), rows(SB_WIDTH), rows(2 * D_MODEL), rows(D_MODEL),
                  wspec, wspec, wspec, pl.BlockSpec((1, D_MODEL), const)],
        out_specs=rows(D_MODEL),
        out_shape=jax.ShapeDtypeStruct((m, D_MODEL), F32),
        scratch_shapes=[pltpu.VMEM((D_MODEL, D_MODEL), BF16)] * 3,
        compiler_params=pltpu.CompilerParams(
            dimension_semantics=("arbitrary",), vmem_limit_bytes=VMEM_LIMIT),
        name="outproj",
    )(oa, ob, sg, gates, x2, wpa, wpb, wo, final_g)


def kernel(x, norm_g, w_in, w_dec_up, b_dec, gla_norm_g, w_pa, w_pb, b_gate, w_o, final_g):
    batch, seq, d = x.shape
    tile, out_rows = GLA_GROUP * GLA_CHUNK, 512
    assert d == D_MODEL and w_in.shape == (D_MODEL, _REF_COLS)
    assert seq % tile == 0 and (batch * seq) % out_rows == 0
    assert (batch * seq // tile) % 2 == 0
    x2 = x.reshape(batch * seq, D_MODEL)

    wdu = jnp.pad(w_dec_up, ((0, RANK_PAD - GLA_RANK), (0, 0))).astype(BF16)

    oa, ob, sg, gates = _mixers(
        x2, norm_g.reshape(1, D_MODEL), _pack_w_in(w_in), b_gate.reshape(1, 2 * D_MODEL),
        wdu, b_dec.reshape(1, GLA_DK), gla_norm_g.reshape(1, GLA_HV), seq)

    out = _out(oa, ob, sg, gates, x2, w_pa, w_pb, w_o, final_g.reshape(1, D_MODEL), tm=out_rows)
    return out.reshape(batch, seq, D_MODEL)
```

```python
import functools

import jax
import jax.numpy as jnp
from jax import lax
from jax.experimental import pallas as pl
from jax.experimental.pallas import tpu as pltpu

D_MODEL = 1024
GLA_HEADS = 4
GLA_HK = 128
GLA_HV = 256
GLA_DK = GLA_HEADS * GLA_HK
GLA_DV = GLA_HEADS * GLA_HV
GLA_RANK = 16
GLA_TAU = 16.0
GLA_CHUNK = 64
SB_HEADS = 8
SB_HD = 128
SB_WIDTH = SB_HEADS * SB_HD
EPS = 1e-6
LOG2E = 1.4426950408889634
MASK_BIAS = -1e30
SB_ZERO_BITS = 160.0

GLA_GROUP = 4

SUBLANES = 8
MXU_COLS = 256
RANK_PAD = MXU_COLS
PACK_COLS = 1024
OUT_ROWS = 256
VMEM_LIMIT = 56 * 1024 * 1024

F32 = jnp.float32
BF16 = jnp.bfloat16

_NT = (((1,), (1,)), ((), ()))
_TN = (((0,), (0,)), ((), ()))


def _sigmoid(x):
    return 1.0 / (1.0 + jnp.exp(-x))


def _softplus(x):
    return jnp.maximum(x, 0.0) + jnp.log(1.0 + jnp.exp(-jnp.abs(x)))


def _sb_block(q, k, v, bias, carry, tri_u):
    z = lax.dot_general(k, q, _NT, preferred_element_type=F32)
    if bias is not None:
        z = z + bias
    sp = jnp.maximum(z, 0.0) + jnp.log(1.0 + jnp.exp2(-jnp.abs(z))) * LOG2E
    x = (z - sp) - jnp.dot(tri_u, sp.astype(BF16), preferred_element_type=F32)
    if carry is not None:
        x = x - carry
    pv = lax.dot_general(v, jnp.exp2(x).astype(BF16), _TN, preferred_element_type=F32)
    return pv, jnp.sum(sp, axis=0, keepdims=True)


def _mixer_kernel(x_ref, ng_ref, wgq, wgk, wgv, wgg, wgr, wsq, wsk, wsv, wsg, wm, bg_ref,
                  wdu_ref, bdec_ref, gng_ref,
                  oa_o, ob_o, sg_o, m_o,
                  st_ref, q_scr, k_scr, v_scr, acc_ref, run_ref, *, tiles_per_seq):
    c = GLA_CHUNK
    t = GLA_GROUP * c
    i = pl.program_id(0)
    last = pl.num_programs(0) - 1
    p = jnp.maximum(i - 1, 0)

    @pl.when(i == 0)
    def _():
        q_scr[...] = jnp.zeros((t, SB_WIDTH), BF16)
        k_scr[pl.ds(0, t), :] = jnp.zeros((t, SB_WIDTH), BF16)
        v_scr[pl.ds(0, t), :] = jnp.zeros((t, SB_WIDTH), BF16)

    @pl.when(i % tiles_per_seq == 0)
    def _():
        st_ref[...] = jnp.zeros_like(st_ref)

    row = lax.broadcasted_iota(jnp.int32, (t, t), 0)
    col = lax.broadcasted_iota(jnp.int32, (t, t), 1)
    tri_u = jnp.where(col > row, 1.0, 0.0).astype(BF16)
    causal_bias = jnp.where(row < col, 0.0, MASK_BIAS)
    bd_tril = (row >= col) & ((row ^ col) < c)
    bd_tril_b = jnp.where(bd_tril, 1.0, 0.0).astype(BF16)

    def sb_head(hd, tile):
        qt = tile % tiles_per_seq
        rows_d = pl.ds(pl.multiple_of(qt * t, t), t)
        rows_p = pl.ds(pl.multiple_of(jnp.maximum(qt - 1, 0) * t, t), t)
        has_prev = (qt > 0).astype(F32)
        hs = slice(hd * SB_HD, (hd + 1) * SB_HD)
        qh = q_scr[:, hs]
        pv_d, rs_d = _sb_block(qh, k_scr[rows_d, hs], v_scr[rows_d, hs], causal_bias, None, tri_u)
        pv_p, rs_p = _sb_block(qh, k_scr[rows_p, hs], v_scr[rows_p, hs], None, rs_d, tri_u)
        acc_ref[:, hs] = jnp.transpose(pv_d + pv_p * has_prev)
        run = rs_d + rs_p * has_prev
        run_ref[hd] = jnp.broadcast_to(run, (SUBLANES, t))
        return run

    def sb_rest(tile, lowest):
        qt = tile % tiles_per_seq

        def more(state):
            kb, low = state
            return (kb >= 0) & (low <= SB_ZERO_BITS)

        def walk(state):
            kb, _ = state
            keys = pl.ds(pl.multiple_of(kb * t, t), t)
            tri_w = jnp.where(col > row, 1.0, 0.0).astype(BF16)
            low = None
            for hd in range(SB_HEADS):
                hs = slice(hd * SB_HD, (hd + 1) * SB_HD)
                carry = run_ref[hd][0:1, :]
                pv, rs = _sb_block(q_scr[:, hs], k_scr[keys, hs], v_scr[keys, hs],
                                   None, carry, tri_w)
                acc_ref[:, hs] += jnp.transpose(pv)
                run = carry + rs
                run_ref[hd] = jnp.broadcast_to(run, (SUBLANES, t))
                low = run if low is None else jnp.minimum(low, run)
            return kb - 1, jnp.min(low)

        lax.while_loop(more, walk, (qt - 2, lowest))
        ob_o[pl.ds(pl.multiple_of((tile % 2) * t, t), t), :] = acc_ref[...].astype(BF16)

    def gla_head(gh, qe, ke, kd, v, gate, lasts):
        ks = slice(gh * GLA_HK, (gh + 1) * GLA_HK)
        vs = slice(gh * GLA_HV, (gh + 1) * GLA_HV)
        attn = lax.dot_general(qe[:, ks], ke[:, ks], _NT, preferred_element_type=F32)
        attn = jnp.where(bd_tril, attn, 0.0).astype(BF16)
        intra = jnp.dot(attn, v[:, vs], preferred_element_type=F32)
        st = st_ref[gh]
        for j in range(GLA_GROUP):
            rows = slice(j * c, (j + 1) * c)
            o = intra[rows] + lax.dot_general(
                qe[rows, ks], st.astype(BF16), _NT, preferred_element_type=F32)
            st = st * jnp.exp(lasts[j][:, ks]) + lax.dot_general(
                v[rows, vs], kd[rows, ks], _TN, preferred_element_type=F32)
            on = o * lax.rsqrt(jnp.mean(o * o, axis=-1, keepdims=True) + EPS) * gng_ref[...]
            oa_o[rows, vs] = (on * gate[rows, vs]).astype(BF16)
        st_ref[gh] = st

    x = x_ref[...]
    h = x * lax.rsqrt(jnp.mean(x * x, axis=-1, keepdims=True) + EPS) * ng_ref[...]
    hb = h.astype(BF16)

    def proj(w_ref):
        return jnp.dot(hb, w_ref[...], preferred_element_type=F32)

    cur = {}

    half = SB_WIDTH // 2

    def piece_sq(j):
        cols = slice(j * half, (j + 1) * half)
        def run_piece():
            cur["sq", j] = (proj(wsq.at[:, cols]) * (SB_HD ** -0.5 * LOG2E)).astype(BF16)
        return run_piece

    def piece_skv(name, w_ref, j):
        cols = slice(j * half, (j + 1) * half)
        def run_piece():
            cur[name, j] = proj(w_ref.at[:, cols]).astype(BF16)
        return run_piece

    def piece_gla_in():
        cur["code"] = proj(wgr).astype(BF16)
        cur["q"] = proj(wgq) * (GLA_HK ** -0.5)
        cur["k"] = proj(wgk)

    def piece_decay():
        u = jnp.dot(cur["code"], wdu_ref[...], preferred_element_type=F32) + bdec_ref[...]
        la = -_softplus(-u) * (1.0 / GLA_TAU)
        la_hi = la.astype(BF16)
        la_lo = (la - la_hi.astype(F32)).astype(BF16)
        cur["bcum"] = (jnp.dot(bd_tril_b, la_hi, preferred_element_type=F32)
                       + jnp.dot(bd_tril_b, la_lo, preferred_element_type=F32))

    def piece_qkd():
        bcum = cur["bcum"]
        cur["v"] = proj(wgv).astype(BF16)
        cur["lasts"] = [bcum[(j + 1) * c - 1:(j + 1) * c, :] for j in range(GLA_GROUP)]
        blast = jnp.concatenate([jnp.broadcast_to(l, (c, GLA_DK)) for l in cur["lasts"]], axis=0)
        cur["qe"] = (cur["q"] * jnp.exp(bcum)).astype(BF16)
        cur["ke"] = (cur["k"] * jnp.exp(-bcum)).astype(BF16)
        cur["kd"] = (cur["k"] * jnp.exp(blast - bcum)).astype(BF16)

    def piece_gate():
        g = proj(wgg)
        cur["gate"] = g * _sigmoid(g)

    def piece_gla(gh):
        return lambda: gla_head(gh, cur["qe"], cur["ke"], cur["kd"], cur["v"], cur["gate"],
                                cur["lasts"])

    def piece_sg(j):
        cols = slice(j * half, (j + 1) * half)
        def run_piece():
            g = proj(wsg.at[:, cols])
            sg_o[:, cols] = (g * _sigmoid(g)).astype(BF16)
        return run_piece

    def piece_m(j):
        cols = slice(j * half, (j + 1) * half)
        def run_piece():
            m_o[:, cols] = _sigmoid(proj(wm.at[:, cols]) + bg_ref[:, cols]).astype(BF16)
        return run_piece

    pieces = [piece_gla_in, piece_decay, piece_qkd, piece_gate,
              piece_gla(0), piece_sq(0), piece_gla(1), piece_sq(1),
              piece_gla(2), piece_skv("sk", wsk, 0), piece_gla(3), piece_skv("sk", wsk, 1),
              piece_skv("sv", wsv, 0), piece_skv("sv", wsv, 1), piece_sg(0), piece_sg(1),
              lambda: (piece_m(0)(), piece_m(1)()), lambda: (piece_m(2)(), piece_m(3)())]

    qt_p = p % tiles_per_seq
    rows_d = pl.ds(pl.multiple_of(qt_p * t, t), t)
    rows_b = pl.ds(pl.multiple_of(jnp.maximum(qt_p - 1, 0) * t, t), t)
    has_prev = (qt_p > 0).astype(F32)
    n_blocks = 2 * SB_HEADS
    blk = [dict() for _ in range(n_blocks)]

    def stage1(n):
        hd, before = divmod(n, 2)
        hs = slice(hd * SB_HD, (hd + 1) * SB_HD)
        rows = rows_b if before else rows_d
        z = lax.dot_general(k_scr[rows, hs], q_scr[:, hs], _NT, preferred_element_type=F32)
        if not before:
            z = z + causal_bias
        sp = jnp.maximum(z, 0.0) + jnp.log(1.0 + jnp.exp2(-jnp.abs(z))) * LOG2E
        blk[n].update(sp=sp.astype(BF16), lb=z - sp, rs=jnp.sum(sp, axis=0, keepdims=True))

    def stage2(n):
        x = blk[n]["lb"] - jnp.dot(tri_u, blk[n]["sp"], preferred_element_type=F32)
        if n % 2:
            x = x - blk[n - 1]["rs"]
        blk[n]["a"] = jnp.exp2(x).astype(BF16)

    def stage3(n):
        hd, before = divmod(n, 2)
        hs = slice(hd * SB_HD, (hd + 1) * SB_HD)
        rows = rows_b if before else rows_d
        blk[n]["pv"] = lax.dot_general(v_scr[rows, hs], blk[n]["a"], _TN,
                                       preferred_element_type=F32)
        if before:
            acc_ref[:, hs] = jnp.transpose(blk[n - 1]["pv"] + blk[n]["pv"] * has_prev)
            run = blk[n - 1]["rs"] + blk[n]["rs"] * has_prev
            run_ref[hd] = jnp.broadcast_to(run, (SUBLANES, t))
            blk[n]["run"] = run

    for n in range(n_blocks + 2):
        if 0 <= n - 2 < n_blocks:
            stage3(n - 2)
        if 0 <= n - 1 < n_blocks:
            stage2(n - 1)
        if n < n_blocks:
            stage1(n)
        if n < len(pieces):
            pieces[n]()
    run_min = blk[1]["run"]
    for hd in range(1, SB_HEADS):
        run_min = jnp.minimum(run_min, blk[2 * hd + 1]["run"])

    sb_rest(p, jnp.min(run_min))

    rows_now = pl.ds(pl.multiple_of((i % tiles_per_seq) * t, t), t)
    for j in range(2):
        cols = slice(j * half, (j + 1) * half)
        q_scr[:, cols] = cur["sq", j]
        k_scr[rows_now, cols] = cur["sk", j]
        v_scr[rows_now, cols] = cur["sv", j]

    @pl.when(i == last)
    def _():
        low = sb_head(0, i)
        for hd in range(1, SB_HEADS):
            low = jnp.minimum(low, sb_head(hd, i))
        sb_rest(i, jnp.min(low))


_REF_HEAD = 2 * GLA_DK + 2 * GLA_DV
_REF_TAIL = _REF_HEAD + GLA_RANK
_REF_M = _REF_TAIL + 4 * SB_WIDTH
_REF_COLS = _REF_M + 2 * D_MODEL
_PK_HEAD = 2 * D_MODEL
_PK_TAIL = _PK_HEAD + _REF_HEAD
_PK_RANK = _PK_TAIL + 4 * SB_WIDTH
_PK_COLS = _PK_RANK + RANK_PAD
_W_BLOCKS = ((GLA_DK, _PK_HEAD // GLA_DK), (GLA_DK, _PK_HEAD // GLA_DK + 1),
             (GLA_DV, (_PK_HEAD + 2 * GLA_DK) // GLA_DV), (GLA_DV, (_PK_HEAD + 2 * GLA_DK) // GLA_DV + 1),
             (RANK_PAD, _PK_RANK // RANK_PAD),
             (SB_WIDTH, _PK_TAIL // SB_WIDTH), (SB_WIDTH, _PK_TAIL // SB_WIDTH + 1),
             (SB_WIDTH, _PK_TAIL // SB_WIDTH + 2), (SB_WIDTH, _PK_TAIL // SB_WIDTH + 3),
             (2 * D_MODEL, 0))


def _pack_kernel(src_ref, wt_ref, wpa_ref, wpb_ref, wo_ref, o_ref, wpa_o, wpb_o, wo_o):
    j = pl.program_id(0)

    @pl.when(j == 0)
    def _():
        wpa_o[...] = wpa_ref[...].astype(BF16)
        wpb_o[...] = wpb_ref[...].astype(BF16)
        wo_o[...] = wo_ref[...].astype(BF16)

    w = wt_ref[...]
    n_row = lax.broadcasted_iota(jnp.int32, w.shape, 0)
    is_code = j == pl.num_programs(0) - 1
    w = jnp.where(is_code & (n_row >= GLA_RANK), 0.0, w).astype(BF16)
    r = lax.broadcasted_iota(jnp.int32, (MXU_COLS, MXU_COLS), 0)
    cidx = lax.broadcasted_iota(jnp.int32, (MXU_COLS, MXU_COLS), 1)
    eye = jnp.where(r == cidx, 1.0, 0.0).astype(BF16)
    for nn in range(0, w.shape[0], MXU_COLS):
        for kk in range(0, w.shape[1], MXU_COLS):
            o_ref[kk:kk + MXU_COLS, nn:nn + MXU_COLS] = lax.dot_general(
                eye, w[nn:nn + MXU_COLS, kk:kk + MXU_COLS], _NT,
                preferred_element_type=F32).astype(BF16)


def _pack_weights(w_in, w_pa, w_pb, w_o):
    d = w_in.shape[0]
    assert w_in.shape[1] == _REF_COLS
    src = ([_REF_M + c for c in range(0, _PK_HEAD, PACK_COLS)]
           + [c for c in range(0, _REF_HEAD, PACK_COLS)]
           + [_REF_TAIL + c for c in range(0, _REF_M - _REF_TAIL, PACK_COLS)]
           + [_REF_HEAD])
    assert (len(src) - 1) * PACK_COLS == _PK_RANK and _REF_HEAD + PACK_COLS <= _REF_COLS
    assert all(c % GLA_RANK == 0 for c in src)
    src = [c // GLA_RANK for c in src]
    grid_spec = pltpu.PrefetchScalarGridSpec(
        num_scalar_prefetch=1,
        grid=(len(src),),
        in_specs=[pl.BlockSpec((pl.Element(PACK_COLS), pl.Element(d)),
                               lambda j, src_ref: (src_ref[j] * GLA_RANK, 0))]
                 + [pl.BlockSpec((d, d), lambda j, src_ref: (0, 0),
                                 pipeline_mode=pl.Buffered(1))] * 3,
        out_specs=[pl.BlockSpec((d, PACK_COLS), lambda j, src_ref: (0, j))]
                  + [pl.BlockSpec((d, d), lambda j, src_ref: (0, 0))] * 3)
    return pl.pallas_call(
        _pack_kernel,
        grid_spec=grid_spec,
        out_shape=[jax.ShapeDtypeStruct((d, _PK_COLS), BF16)]
                  + [jax.ShapeDtypeStruct((d, d), BF16)] * 3,
        compiler_params=pltpu.CompilerParams(
            dimension_semantics=("arbitrary",), vmem_limit_bytes=VMEM_LIMIT),
        name="packw",
    )(jnp.asarray(src, jnp.int32), w_in.T, w_pa, w_pb, w_o)


def _mixers(x2, norm_g, w_all, b_gate, wdu, b_dec, gla_norm_g, seq):
    m = x2.shape[0]
    tm = GLA_GROUP * GLA_CHUNK
    const = lambda i: (0, 0)
    w_specs = [pl.BlockSpec((D_MODEL, n), functools.partial(lambda i, c: (0, c), c=c),
                            pipeline_mode=pl.Buffered(1))
               for n, c in _W_BLOCKS]
    in_specs = ([pl.BlockSpec((tm, D_MODEL), lambda i: (i, 0)),
                 pl.BlockSpec((1, D_MODEL), const)]
                + w_specs
                + [pl.BlockSpec((1, 2 * D_MODEL), const),
                   pl.BlockSpec((RANK_PAD, GLA_DK), const),
                   pl.BlockSpec((1, GLA_DK), const),
                   pl.BlockSpec((1, GLA_HV), const)])
    out_widths = [GLA_DV, SB_WIDTH, SB_WIDTH, 2 * D_MODEL]
    out_specs = [pl.BlockSpec((tm, n), lambda i: (i, 0)) for n in out_widths]
    out_specs[1] = pl.BlockSpec((2 * tm, SB_WIDTH), lambda i: (jnp.maximum(i - 1, 0) // 2, 0))
    out_shape = [jax.ShapeDtypeStruct((m, n), BF16) for n in out_widths]
    return pl.pallas_call(
        functools.partial(_mixer_kernel, tiles_per_seq=seq // tm),
        grid=(m // tm,),
        in_specs=in_specs,
        out_specs=out_specs,
        out_shape=out_shape,
        scratch_shapes=[
            pltpu.VMEM((GLA_HEADS, GLA_HV, GLA_HK), F32),
            pltpu.VMEM((tm, SB_WIDTH), BF16),
            pltpu.VMEM((seq, SB_WIDTH), BF16),
            pltpu.VMEM((seq, SB_WIDTH), BF16),
            pltpu.VMEM((tm, SB_WIDTH), F32),
            pltpu.VMEM((SB_HEADS, SUBLANES, tm), F32),
        ],
        compiler_params=pltpu.CompilerParams(
            dimension_semantics=("arbitrary",), vmem_limit_bytes=VMEM_LIMIT),
        name="mixers",
    )(x2, norm_g, *([w_all] * len(_W_BLOCKS)), b_gate, wdu, b_dec, gla_norm_g)


def _out_kernel(oa_ref, ob_ref, sg_ref, m_ref, x_ref, wpa_b, wpb_b, wo_b, fg_ref, o_ref):
    tm = o_ref.shape[0]
    groups = [pl.ds(r, OUT_ROWS) for r in range(0, tm, OUT_ROWS)]

    def branches(rows):
        ya = jnp.dot(oa_ref[rows, :], wpa_b[...], preferred_element_type=F32)
        yb = jnp.dot(ob_ref[rows, :] * sg_ref[rows, :], wpb_b[...], preferred_element_type=F32)
        return ya, yb

    def merge(rows, ya, yb):
        merged = (m_ref[rows, :D_MODEL].astype(F32) * ya + m_ref[rows, D_MODEL:].astype(F32) * yb)
        return x_ref[rows, :] + jnp.dot(merged.astype(BF16), wo_b[...],
                                        preferred_element_type=F32)

    def finish(rows, y):
        o_ref[rows, :] = y * lax.rsqrt(jnp.mean(y * y, axis=-1, keepdims=True) + EPS) * fg_ref[...]

    yab = branches(groups[0])
    y_prev = None
    for g, rows in enumerate(groups):
        yab_next = branches(groups[g + 1]) if g + 1 < len(groups) else None
        y = merge(rows, *yab)
        if y_prev is not None:
            finish(groups[g - 1], y_prev)
        yab, y_prev = yab_next, y
    finish(groups[-1], y_prev)


def _out(oa, ob, sg, gates, x2, wpa, wpb, wo, final_g, tm):
    m = x2.shape[0]
    const = lambda i: (0, 0)
    rows = lambda n: pl.BlockSpec((tm, n), lambda i: (i, 0))
    wspec = pl.BlockSpec((D_MODEL, D_MODEL), const, pipeline_mode=pl.Buffered(1))
    return pl.pallas_call(
        _out_kernel,
        grid=(m // tm,),
        in_specs=[rows(GLA_DV), rows(SB_WIDTH), rows(SB_WIDTH), rows(2 * D_MODEL), rows(D_MODEL),
                  wspec, wspec, wspec, pl.BlockSpec((1, D_MODEL), const)],
        out_specs=rows(D_MODEL),
        out_shape=jax.ShapeDtypeStruct((m, D_MODEL), F32),
        compiler_params=pltpu.CompilerParams(
            dimension_semantics=("parallel",), vmem_limit_bytes=VMEM_LIMIT),
        name="outproj",
    )(oa, ob, sg, gates, x2, wpa, wpb, wo, final_g)


def kernel(x, norm_g, w_in, w_dec_up, b_dec, gla_norm_g, w_pa, w_pb, b_gate, w_o, final_g):
    batch, seq, d = x.shape
    tile, out_rows = GLA_GROUP * GLA_CHUNK, 1024
    assert d == D_MODEL and w_in.shape == (D_MODEL, _REF_COLS)
    assert seq % tile == 0 and (batch * seq) % out_rows == 0
    assert (batch * seq // tile) % 2 == 0
    x2 = x.reshape(batch * seq, D_MODEL)

    wdu = jnp.pad(w_dec_up, ((0, RANK_PAD - GLA_RANK), (0, 0))).astype(BF16)

    w_all, wpa, wpb, wo = _pack_weights(w_in, w_pa, w_pb, w_o)
    oa, ob, sg, gates = _mixers(
        x2, norm_g.reshape(1, D_MODEL), w_all, b_gate.reshape(1, 2 * D_MODEL),
        wdu, b_dec.reshape(1, GLA_DK), gla_norm_g.reshape(1, GLA_HV), seq)

    out = _out(oa, ob, sg, gates, x2, wpa, wpb, wo, final_g.reshape(1, D_MODEL), tm=out_rows)
    return out.reshape(batch, seq, D_MODEL)
```

```python
import functools

import jax
import jax.numpy as jnp
from jax import lax
from jax.experimental import pallas as pl
from jax.experimental.pallas import tpu as pltpu

D_MODEL = 1024
GLA_HEADS = 4
GLA_HK = 128
GLA_HV = 256
GLA_DK = GLA_HEADS * GLA_HK
GLA_DV = GLA_HEADS * GLA_HV
GLA_RANK = 16
GLA_TAU = 16.0
GLA_CHUNK = 64
SB_HEADS = 8
SB_HD = 128
SB_WIDTH = SB_HEADS * SB_HD
EPS = 1e-6
LOG2E = 1.4426950408889634
MASK_BIAS = -1e30
SB_ZERO_BITS = 160.0

GLA_GROUP = 4

SUBLANES = 8
MXU_COLS = 256
RANK_PAD = MXU_COLS
PACK_COLS = 1024
OUT_ROWS = 256
VMEM_LIMIT = 56 * 1024 * 1024

F32 = jnp.float32
BF16 = jnp.bfloat16

_NT = (((1,), (1,)), ((), ()))
_TN = (((0,), (0,)), ((), ()))


def _sigmoid(x):
    return 1.0 / (1.0 + jnp.exp(-x))


def _softplus(x):
    return jnp.maximum(x, 0.0) + jnp.log(1.0 + jnp.exp(-jnp.abs(x)))


def _sb_block(q, k, v, bias, carry, tri_u):
    z = lax.dot_general(k, q, _NT, preferred_element_type=F32)
    if bias is not None:
        z = z + bias
    sp = jnp.maximum(z, 0.0) + jnp.log(1.0 + jnp.exp2(-jnp.abs(z))) * LOG2E
    x = (z - sp) - jnp.dot(tri_u, sp.astype(BF16), preferred_element_type=F32)
    if carry is not None:
        x = x - carry
    pv = lax.dot_general(v, jnp.exp2(x).astype(BF16), _TN, preferred_element_type=F32)
    return pv, jnp.sum(sp, axis=0, keepdims=True)


def _mixer_kernel(x_ref, ng_ref, wgq, wgk, wgv, wgg, wgr, wsq, wsk, wsv, wsg, wm, bg_ref,
                  wdu_ref, bdec_ref, gng_ref,
                  oa_o, ob_o, sg_o, m_o,
                  st_ref, q_scr, k_scr, v_scr, acc_ref, run_ref, *, tiles_per_seq):
    c = GLA_CHUNK
    t = GLA_GROUP * c
    i = pl.program_id(0)
    last = pl.num_programs(0) - 1
    p = jnp.maximum(i - 1, 0)

    @pl.when(i == 0)
    def _():
        q_scr[...] = jnp.zeros((t, SB_WIDTH), BF16)
        k_scr[pl.ds(0, t), :] = jnp.zeros((t, SB_WIDTH), BF16)
        v_scr[pl.ds(0, t), :] = jnp.zeros((t, SB_WIDTH), BF16)

    @pl.when(i % tiles_per_seq == 0)
    def _():
        st_ref[...] = jnp.zeros_like(st_ref)

    row = lax.broadcasted_iota(jnp.int32, (t, t), 0)
    col = lax.broadcasted_iota(jnp.int32, (t, t), 1)
    tri_u = jnp.where(col > row, 1.0, 0.0).astype(BF16)
    causal_bias = jnp.where(row < col, 0.0, MASK_BIAS)
    bd_tril = (row >= col) & ((row ^ col) < c)
    bd_tril_b = jnp.where(bd_tril, 1.0, 0.0).astype(BF16)

    def sb_head(hd, tile):
        qt = tile % tiles_per_seq
        rows_d = pl.ds(pl.multiple_of(qt * t, t), t)
        rows_p = pl.ds(pl.multiple_of(jnp.maximum(qt - 1, 0) * t, t), t)
        has_prev = (qt > 0).astype(F32)
        hs = slice(hd * SB_HD, (hd + 1) * SB_HD)
        qh = q_scr[:, hs]
        pv_d, rs_d = _sb_block(qh, k_scr[rows_d, hs], v_scr[rows_d, hs], causal_bias, None, tri_u)
        pv_p, rs_p = _sb_block(qh, k_scr[rows_p, hs], v_scr[rows_p, hs], None, rs_d, tri_u)
        acc_ref[:, hs] = jnp.transpose(pv_d + pv_p * has_prev)
        run = rs_d + rs_p * has_prev
        run_ref[hd] = jnp.broadcast_to(run, (SUBLANES, t))
        return run

    def sb_rest(tile, lowest):
        qt = tile % tiles_per_seq

        def more(state):
            kb, low = state
            return (kb >= 0) & (low <= SB_ZERO_BITS)

        def walk(state):
            kb, _ = state
            keys = pl.ds(pl.multiple_of(kb * t, t), t)
            tri_w = jnp.where(col > row, 1.0, 0.0).astype(BF16)
            low = None
            for hd in range(SB_HEADS):
                hs = slice(hd * SB_HD, (hd + 1) * SB_HD)
                carry = run_ref[hd][0:1, :]
                pv, rs = _sb_block(q_scr[:, hs], k_scr[keys, hs], v_scr[keys, hs],
                                   None, carry, tri_w)
                acc_ref[:, hs] += jnp.transpose(pv)
                run = carry + rs
                run_ref[hd] = jnp.broadcast_to(run, (SUBLANES, t))
                low = run if low is None else jnp.minimum(low, run)
            return kb - 1, jnp.min(low)

        lax.while_loop(more, walk, (qt - 2, lowest))
        ob_o[pl.ds(pl.multiple_of((tile % 2) * t, t), t), :] = acc_ref[...].astype(BF16)

    def gla_head(gh, qe, ke, kd, v, gate, lasts):
        ks = slice(gh * GLA_HK, (gh + 1) * GLA_HK)
        vs = slice(gh * GLA_HV, (gh + 1) * GLA_HV)
        attn = lax.dot_general(qe[:, ks], ke[:, ks], _NT, preferred_element_type=F32)
        attn = jnp.where(bd_tril, attn, 0.0).astype(BF16)
        intra = jnp.dot(attn, v[:, vs], preferred_element_type=F32)
        st = st_ref[gh]
        for j in range(GLA_GROUP):
            rows = slice(j * c, (j + 1) * c)
            o = intra[rows] + lax.dot_general(
                qe[rows, ks], st.astype(BF16), _NT, preferred_element_type=F32)
            st = st * jnp.exp(lasts[j][:, ks]) + lax.dot_general(
                v[rows, vs], kd[rows, ks], _TN, preferred_element_type=F32)
            on = o * lax.rsqrt(jnp.mean(o * o, axis=-1, keepdims=True) + EPS) * gng_ref[...]
            oa_o[rows, vs] = (on * gate[rows, vs]).astype(BF16)
        st_ref[gh] = st

    x = x_ref[...]
    h = x * lax.rsqrt(jnp.mean(x * x, axis=-1, keepdims=True) + EPS) * ng_ref[...]
    hb = h.astype(BF16)

    def proj(w_ref):
        return jnp.dot(hb, w_ref[...], preferred_element_type=F32)

    cur = {}

    half = SB_WIDTH // 2

    def piece_sq(j):
        cols = slice(j * half, (j + 1) * half)
        def run_piece():
            cur["sq", j] = (proj(wsq.at[:, cols]) * (SB_HD ** -0.5 * LOG2E)).astype(BF16)
        return run_piece

    def piece_skv(name, w_ref, j):
        cols = slice(j * half, (j + 1) * half)
        def run_piece():
            cur[name, j] = proj(w_ref.at[:, cols]).astype(BF16)
        return run_piece

    def piece_gla_in():
        cur["code"] = proj(wgr).astype(BF16)
        cur["q"] = proj(wgq) * (GLA_HK ** -0.5)
        cur["k"] = proj(wgk)

    def piece_decay():
        u = jnp.dot(cur["code"], wdu_ref[...], preferred_element_type=F32) + bdec_ref[...]
        la = -_softplus(-u) * (1.0 / GLA_TAU)
        cur["bcum"] = jnp.dot(bd_tril_b, la.astype(BF16), preferred_element_type=F32)

    def piece_qkd():
        bcum = cur["bcum"]
        cur["v"] = proj(wgv).astype(BF16)
        cur["lasts"] = [bcum[(j + 1) * c - 1:(j + 1) * c, :] for j in range(GLA_GROUP)]
        blast = jnp.concatenate([jnp.broadcast_to(l, (c, GLA_DK)) for l in cur["lasts"]], axis=0)
        cur["qe"] = (cur["q"] * jnp.exp(bcum)).astype(BF16)
        cur["ke"] = (cur["k"] * jnp.exp(-bcum)).astype(BF16)
        cur["kd"] = (cur["k"] * jnp.exp(blast - bcum)).astype(BF16)

    def piece_gate():
        g = proj(wgg)
        cur["gate"] = g * _sigmoid(g)

    def piece_gla(gh):
        return lambda: gla_head(gh, cur["qe"], cur["ke"], cur["kd"], cur["v"], cur["gate"],
                                cur["lasts"])

    def piece_sg(j):
        cols = slice(j * half, (j + 1) * half)
        def run_piece():
            g = proj(wsg.at[:, cols])
            sg_o[:, cols] = (g * _sigmoid(g)).astype(BF16)
        return run_piece

    def piece_m(j):
        cols = slice(j * half, (j + 1) * half)
        def run_piece():
            m_o[:, cols] = _sigmoid(proj(wm.at[:, cols]) + bg_ref[:, cols]).astype(BF16)
        return run_piece

    pieces = [piece_gla_in, piece_decay, piece_qkd, piece_gate,
              piece_gla(0), piece_sq(0), piece_gla(1), piece_sq(1),
              piece_gla(2), piece_skv("sk", wsk, 0), piece_gla(3), piece_skv("sk", wsk, 1),
              piece_skv("sv", wsv, 0), piece_skv("sv", wsv, 1), piece_sg(0), piece_sg(1),
              lambda: (piece_m(0)(), piece_m(1)()), lambda: (piece_m(2)(), piece_m(3)())]

    qt_p = p % tiles_per_seq
    rows_d = pl.ds(pl.multiple_of(qt_p * t, t), t)
    rows_b = pl.ds(pl.multiple_of(jnp.maximum(qt_p - 1, 0) * t, t), t)
    has_prev = (qt_p > 0).astype(F32)
    n_blocks = 2 * SB_HEADS
    blk = [dict() for _ in range(n_blocks)]

    def stage1(n):
        hd, before = divmod(n, 2)
        hs = slice(hd * SB_HD, (hd + 1) * SB_HD)
        rows = rows_b if before else rows_d
        z = lax.dot_general(k_scr[rows, hs], q_scr[:, hs], _NT, preferred_element_type=F32)
        if not before:
            z = z + causal_bias
        sp = jnp.maximum(z, 0.0) + jnp.log(1.0 + jnp.exp2(-jnp.abs(z))) * LOG2E
        blk[n].update(sp=sp.astype(BF16), lb=z - sp, rs=jnp.sum(sp, axis=0, keepdims=True))

    def stage2(n):
        x = blk[n]["lb"] - jnp.dot(tri_u, blk[n]["sp"], preferred_element_type=F32)
        if n % 2:
            x = x - blk[n - 1]["rs"]
        blk[n]["a"] = jnp.exp2(x).astype(BF16)

    def stage3(n):
        hd, before = divmod(n, 2)
        hs = slice(hd * SB_HD, (hd + 1) * SB_HD)
        rows = rows_b if before else rows_d
        blk[n]["pv"] = lax.dot_general(v_scr[rows, hs], blk[n]["a"], _TN,
                                       preferred_element_type=F32)
        if before:
            acc_ref[:, hs] = jnp.transpose(blk[n - 1]["pv"] + blk[n]["pv"] * has_prev)
            run = blk[n - 1]["rs"] + blk[n]["rs"] * has_prev
            run_ref[hd] = jnp.broadcast_to(run, (SUBLANES, t))
            blk[n]["run"] = run

    for n in range(n_blocks + 2):
        if 0 <= n - 2 < n_blocks:
            stage3(n - 2)
        if 0 <= n - 1 < n_blocks:
            stage2(n - 1)
        if n < n_blocks:
            stage1(n)
        if n < len(pieces):
            pieces[n]()
    run_min = blk[1]["run"]
    for hd in range(1, SB_HEADS):
        run_min = jnp.minimum(run_min, blk[2 * hd + 1]["run"])

    sb_rest(p, jnp.min(run_min))

    rows_now = pl.ds(pl.multiple_of((i % tiles_per_seq) * t, t), t)
    for j in range(2):
        cols = slice(j * half, (j + 1) * half)
        q_scr[:, cols] = cur["sq", j]
        k_scr[rows_now, cols] = cur["sk", j]
        v_scr[rows_now, cols] = cur["sv", j]

    @pl.when(i == last)
    def _():
        low = sb_head(0, i)
        for hd in range(1, SB_HEADS):
            low = jnp.minimum(low, sb_head(hd, i))
        sb_rest(i, jnp.min(low))


_REF_HEAD = 2 * GLA_DK + 2 * GLA_DV
_REF_TAIL = _REF_HEAD + GLA_RANK
_REF_M = _REF_TAIL + 4 * SB_WIDTH
_REF_COLS = _REF_M + 2 * D_MODEL
_PK_HEAD = 2 * D_MODEL
_PK_TAIL = _PK_HEAD + _REF_HEAD
_PK_RANK = _PK_TAIL + 4 * SB_WIDTH
_PK_COLS = _PK_RANK + RANK_PAD
_W_BLOCKS = ((GLA_DK, _PK_HEAD // GLA_DK), (GLA_DK, _PK_HEAD // GLA_DK + 1),
             (GLA_DV, (_PK_HEAD + 2 * GLA_DK) // GLA_DV), (GLA_DV, (_PK_HEAD + 2 * GLA_DK) // GLA_DV + 1),
             (RANK_PAD, _PK_RANK // RANK_PAD),
             (SB_WIDTH, _PK_TAIL // SB_WIDTH), (SB_WIDTH, _PK_TAIL // SB_WIDTH + 1),
             (SB_WIDTH, _PK_TAIL // SB_WIDTH + 2), (SB_WIDTH, _PK_TAIL // SB_WIDTH + 3),
             (2 * D_MODEL, 0))


def _pack_kernel(src_ref, wt_ref, wpa_ref, wpb_ref, wo_ref, o_ref, wpa_o, wpb_o, wo_o):
    j = pl.program_id(0)

    @pl.when(j == 0)
    def _():
        wpa_o[...] = wpa_ref[...].astype(BF16)
        wpb_o[...] = wpb_ref[...].astype(BF16)
        wo_o[...] = wo_ref[...].astype(BF16)

    w = wt_ref[...]
    n_row = lax.broadcasted_iota(jnp.int32, w.shape, 0)
    is_code = j == pl.num_programs(0) - 1
    w = jnp.where(is_code & (n_row >= GLA_RANK), 0.0, w).astype(BF16)
    r = lax.broadcasted_iota(jnp.int32, (MXU_COLS, MXU_COLS), 0)
    cidx = lax.broadcasted_iota(jnp.int32, (MXU_COLS, MXU_COLS), 1)
    eye = jnp.where(r == cidx, 1.0, 0.0).astype(BF16)
    for nn in range(0, w.shape[0], MXU_COLS):
        for kk in range(0, w.shape[1], MXU_COLS):
            o_ref[kk:kk + MXU_COLS, nn:nn + MXU_COLS] = lax.dot_general(
                eye, w[nn:nn + MXU_COLS, kk:kk + MXU_COLS], _NT,
                preferred_element_type=F32).astype(BF16)


def _pack_weights(w_in, w_pa, w_pb, w_o):
    d = w_in.shape[0]
    assert w_in.shape[1] == _REF_COLS
    src = ([_REF_M + c for c in range(0, _PK_HEAD, PACK_COLS)]
           + [c for c in range(0, _REF_HEAD, PACK_COLS)]
           + [_REF_TAIL + c for c in range(0, _REF_M - _REF_TAIL, PACK_COLS)]
           + [_REF_HEAD])
    assert (len(src) - 1) * PACK_COLS == _PK_RANK and _REF_HEAD + PACK_COLS <= _REF_COLS
    assert all(c % GLA_RANK == 0 for c in src)
    src = [c // GLA_RANK for c in src]
    grid_spec = pltpu.PrefetchScalarGridSpec(
        num_scalar_prefetch=1,
        grid=(len(src),),
        in_specs=[pl.BlockSpec((pl.Element(PACK_COLS), pl.Element(d)),
                               lambda j, src_ref: (src_ref[j] * GLA_RANK, 0))]
                 + [pl.BlockSpec((d, d), lambda j, src_ref: (0, 0),
                                 pipeline_mode=pl.Buffered(1))] * 3,
        out_specs=[pl.BlockSpec((d, PACK_COLS), lambda j, src_ref: (0, j))]
                  + [pl.BlockSpec((d, d), lambda j, src_ref: (0, 0))] * 3)
    return pl.pallas_call(
        _pack_kernel,
        grid_spec=grid_spec,
        out_shape=[jax.ShapeDtypeStruct((d, _PK_COLS), BF16)]
                  + [jax.ShapeDtypeStruct((d, d), BF16)] * 3,
        compiler_params=pltpu.CompilerParams(
            dimension_semantics=("arbitrary",), vmem_limit_bytes=VMEM_LIMIT),
        name="packw",
    )(jnp.asarray(src, jnp.int32), w_in.T, w_pa, w_pb, w_o)


def _mixers(x2, norm_g, w_all, b_gate, wdu, b_dec, gla_norm_g, seq):
    m = x2.shape[0]
    tm = GLA_GROUP * GLA_CHUNK
    const = lambda i: (0, 0)
    w_specs = [pl.BlockSpec((D_MODEL, n), functools.partial(lambda i, c: (0, c), c=c),
                            pipeline_mode=pl.Buffered(1))
               for n, c in _W_BLOCKS]
    in_specs = ([pl.BlockSpec((tm, D_MODEL), lambda i: (i, 0)),
                 pl.BlockSpec((1, D_MODEL), const)]
                + w_specs
                + [pl.BlockSpec((1, 2 * D_MODEL), const),
                   pl.BlockSpec((RANK_PAD, GLA_DK), const),
                   pl.BlockSpec((1, GLA_DK), const),
                   pl.BlockSpec((1, GLA_HV), const)])
    out_widths = [GLA_DV, SB_WIDTH, SB_WIDTH, 2 * D_MODEL]
    out_specs = [pl.BlockSpec((tm, n), lambda i: (i, 0)) for n in out_widths]
    out_specs[1] = pl.BlockSpec((2 * tm, SB_WIDTH), lambda i: (jnp.maximum(i - 1, 0) // 2, 0))
    out_shape = [jax.ShapeDtypeStruct((m, n), BF16) for n in out_widths]
    return pl.pallas_call(
        functools.partial(_mixer_kernel, tiles_per_seq=seq // tm),
        grid=(m // tm,),
        in_specs=in_specs,
        out_specs=out_specs,
        out_shape=out_shape,
        scratch_shapes=[
            pltpu.VMEM((GLA_HEADS, GLA_HV, GLA_HK), F32),
            pltpu.VMEM((tm, SB_WIDTH), BF16),
            pltpu.VMEM((seq, SB_WIDTH), BF16),
            pltpu.VMEM((seq, SB_WIDTH), BF16),
            pltpu.VMEM((tm, SB_WIDTH), F32),
            pltpu.VMEM((SB_HEADS, SUBLANES, tm), F32),
        ],
        compiler_params=pltpu.CompilerParams(
            dimension_semantics=("arbitrary",), vmem_limit_bytes=VMEM_LIMIT),
        name="mixers",
    )(x2, norm_g, *([w_all] * len(_W_BLOCKS)), b_gate, wdu, b_dec, gla_norm_g)


def _out_kernel(oa_ref, ob_ref, sg_ref, m_ref, x_ref, wpa_b, wpb_b, wo_b, fg_ref, o_ref):
    tm = o_ref.shape[0]
    groups = [pl.ds(r, OUT_ROWS) for r in range(0, tm, OUT_ROWS)]

    def branches(rows):
        ya = jnp.dot(oa_ref[rows, :], wpa_b[...], preferred_element_type=F32)
        yb = jnp.dot(ob_ref[rows, :] * sg_ref[rows, :], wpb_b[...], preferred_element_type=F32)
        return ya, yb

    def merge(rows, ya, yb):
        merged = (m_ref[rows, :D_MODEL].astype(F32) * ya + m_ref[rows, D_MODEL:].astype(F32) * yb)
        return x_ref[rows, :] + jnp.dot(merged.astype(BF16), wo_b[...],
                                        preferred_element_type=F32)

    def finish(rows, y):
        o_ref[rows, :] = y * lax.rsqrt(jnp.mean(y * y, axis=-1, keepdims=True) + EPS) * fg_ref[...]

    yab = branches(groups[0])
    y_prev = None
    for g, rows in enumerate(groups):
        yab_next = branches(groups[g + 1]) if g + 1 < len(groups) else None
        y = merge(rows, *yab)
        if y_prev is not None:
            finish(groups[g - 1], y_prev)
        yab, y_prev = yab_next, y
    finish(groups[-1], y_prev)


def _out(oa, ob, sg, gates, x2, wpa, wpb, wo, final_g, tm):
    m = x2.shape[0]
    const = lambda i: (0, 0)
    rows = lambda n: pl.BlockSpec((tm, n), lambda i: (i, 0))
    wspec = pl.BlockSpec((D_MODEL, D_MODEL), const, pipeline_mode=pl.Buffered(1))
    return pl.pallas_call(
        _out_kernel,
        grid=(m // tm,),
        in_specs=[rows(GLA_DV), rows(SB_WIDTH), rows(SB_WIDTH), rows(2 * D_MODEL), rows(D_MODEL),
                  wspec, wspec, wspec, pl.BlockSpec((1, D_MODEL), const)],
        out_specs=rows(D_MODEL),
        out_shape=jax.ShapeDtypeStruct((m, D_MODEL), F32),
        compiler_params=pltpu.CompilerParams(
            dimension_semantics=("parallel",), vmem_limit_bytes=VMEM_LIMIT),
        name="outproj",
    )(oa, ob, sg, gates, x2, wpa, wpb, wo, final_g)


def kernel(x, norm_g, w_in, w_dec_up, b_dec, gla_norm_g, w_pa, w_pb, b_gate, w_o, final_g):
    batch, seq, d = x.shape
    tile, out_rows = GLA_GROUP * GLA_CHUNK, 1024
    assert d == D_MODEL and w_in.shape == (D_MODEL, _REF_COLS)
    assert seq % tile == 0 and (batch * seq) % out_rows == 0
    assert (batch * seq // tile) % 2 == 0
    x2 = x.reshape(batch * seq, D_MODEL)

    wdu = jnp.pad(w_dec_up, ((0, RANK_PAD - GLA_RANK), (0, 0))).astype(BF16)

    w_all, wpa, wpb, wo = _pack_weights(w_in, w_pa, w_pb, w_o)
    oa, ob, sg, gates = _mixers(
        x2, norm_g.reshape(1, D_MODEL), w_all, b_gate.reshape(1, 2 * D_MODEL),
        wdu, b_dec.reshape(1, GLA_DK), gla_norm_g.reshape(1, GLA_HV), seq)

    out = _out(oa, ob, sg, gates, x2, wpa, wpb, wo, final_g.reshape(1, D_MODEL), tm=out_rows)
    return out.reshape(batch, seq, D_MODEL)
```

```python
import functools

import jax
import jax.numpy as jnp
from jax import lax
from jax.experimental import pallas as pl
from jax.experimental.pallas import tpu as pltpu

D_MODEL = 1024
GLA_HEADS = 4
GLA_HK = 128
GLA_HV = 256
GLA_DK = GLA_HEADS * GLA_HK
GLA_DV = GLA_HEADS * GLA_HV
GLA_RANK = 16
GLA_TAU = 16.0
GLA_CHUNK = 64
SB_HEADS = 8
SB_HD = 128
SB_WIDTH = SB_HEADS * SB_HD
EPS = 1e-6
LOG2E = 1.4426950408889634
MASK_BIAS = -1e30
SB_ZERO_BITS = 160.0

GLA_GROUP = 4

SUBLANES = 8
MXU_COLS = 256
RANK_PAD = MXU_COLS
PACK_COLS = 1024
OUT_ROWS = 256
VMEM_LIMIT = 56 * 1024 * 1024

F32 = jnp.float32
BF16 = jnp.bfloat16

_NT = (((1,), (1,)), ((), ()))
_TN = (((0,), (0,)), ((), ()))


def _sigmoid(x):
    return 1.0 / (1.0 + jnp.exp(-x))


def _softplus(x):
    return jnp.maximum(x, 0.0) + jnp.log(1.0 + jnp.exp(-jnp.abs(x)))


def _sb_block(q, k, v, bias, carry, tri_u):
    z = lax.dot_general(k, q, _NT, preferred_element_type=F32)
    if bias is not None:
        z = z + bias
    sp = jnp.maximum(z, 0.0) + jnp.log(1.0 + jnp.exp2(-jnp.abs(z))) * LOG2E
    x = (z - sp) - jnp.dot(tri_u, sp.astype(BF16), preferred_element_type=F32)
    if carry is not None:
        x = x - carry
    pv = lax.dot_general(v, jnp.exp2(x).astype(BF16), _TN, preferred_element_type=F32)
    return pv, jnp.sum(sp, axis=0, keepdims=True)


def _mixer_step(i, last, x_ref, ng_ref, w_scr, bg_ref, wdu_ref, bdec_ref, gng_ref,
                oa_o, ob_o, sg_o, m_o,
                st_ref, q_scr, k_scr, v_scr, acc_ref, run_ref, tiles_per_seq):
    c = GLA_CHUNK
    t = GLA_GROUP * c
    p = jnp.maximum(i - 1, 0)

    def wv(off, width):
        lo = off % PACK_COLS
        return w_scr.at[off // PACK_COLS, :, lo:lo + width]

    wgq, wgk = wv(_PK_HEAD, GLA_DK), wv(_PK_HEAD + GLA_DK, GLA_DK)
    wgv, wgg = wv(_PK_HEAD + 2 * GLA_DK, GLA_DV), wv(_PK_HEAD + 2 * GLA_DK + GLA_DV, GLA_DV)
    wgr = wv(_PK_RANK, RANK_PAD)
    sq_off, sk_off, sv_off, sg_off = (_PK_TAIL + n * SB_WIDTH for n in range(4))

    @pl.when(i == 0)
    def _():
        q_scr[...] = jnp.zeros((t, SB_WIDTH), BF16)
        k_scr[pl.ds(0, t), :] = jnp.zeros((t, SB_WIDTH), BF16)
        v_scr[pl.ds(0, t), :] = jnp.zeros((t, SB_WIDTH), BF16)

    @pl.when(i % tiles_per_seq == 0)
    def _():
        st_ref[...] = jnp.zeros_like(st_ref)

    row = lax.broadcasted_iota(jnp.int32, (t, t), 0)
    col = lax.broadcasted_iota(jnp.int32, (t, t), 1)
    tri_u = jnp.where(col > row, 1.0, 0.0).astype(BF16)
    causal_bias = jnp.where(row < col, 0.0, MASK_BIAS)
    bd_tril = (row >= col) & ((row ^ col) < c)
    bd_tril_b = jnp.where(bd_tril, 1.0, 0.0).astype(BF16)

    def sb_head(hd, tile):
        qt = tile % tiles_per_seq
        rows_d = pl.ds(pl.multiple_of(qt * t, t), t)
        rows_p = pl.ds(pl.multiple_of(jnp.maximum(qt - 1, 0) * t, t), t)
        has_prev = (qt > 0).astype(F32)
        hs = slice(hd * SB_HD, (hd + 1) * SB_HD)
        qh = q_scr[:, hs]
        pv_d, rs_d = _sb_block(qh, k_scr[rows_d, hs], v_scr[rows_d, hs], causal_bias, None, tri_u)
        pv_p, rs_p = _sb_block(qh, k_scr[rows_p, hs], v_scr[rows_p, hs], None, rs_d, tri_u)
        acc_ref[:, hs] = jnp.transpose(pv_d + pv_p * has_prev)
        run = rs_d + rs_p * has_prev
        run_ref[hd] = jnp.broadcast_to(run, (SUBLANES, t))
        return run

    def sb_rest(tile, lowest):
        qt = tile % tiles_per_seq

        def more(state):
            kb, low = state
            return (kb >= 0) & (low <= SB_ZERO_BITS)

        def walk(state):
            kb, _ = state
            keys = pl.ds(pl.multiple_of(kb * t, t), t)
            tri_w = jnp.where(col > row, 1.0, 0.0).astype(BF16)
            low = None
            for hd in range(SB_HEADS):
                hs = slice(hd * SB_HD, (hd + 1) * SB_HD)
                carry = run_ref[hd][0:1, :]
                pv, rs = _sb_block(q_scr[:, hs], k_scr[keys, hs], v_scr[keys, hs],
                                   None, carry, tri_w)
                acc_ref[:, hs] += jnp.transpose(pv)
                run = carry + rs
                run_ref[hd] = jnp.broadcast_to(run, (SUBLANES, t))
                low = run if low is None else jnp.minimum(low, run)
            return kb - 1, jnp.min(low)

        lax.while_loop(more, walk, (qt - 2, lowest))
        ob_o[pl.ds(pl.multiple_of((tile % 2) * t, t), t), :] = acc_ref[...].astype(BF16)

    def gla_head(gh, qe, ke, kd, v, gate, lasts):
        ks = slice(gh * GLA_HK, (gh + 1) * GLA_HK)
        vs = slice(gh * GLA_HV, (gh + 1) * GLA_HV)
        attn = lax.dot_general(qe[:, ks], ke[:, ks], _NT, preferred_element_type=F32)
        attn = jnp.where(bd_tril, attn, 0.0).astype(BF16)
        intra = jnp.dot(attn, v[:, vs], preferred_element_type=F32)
        st = st_ref[gh]
        for j in range(GLA_GROUP):
            rows = slice(j * c, (j + 1) * c)
            o = intra[rows] + lax.dot_general(
                qe[rows, ks], st.astype(BF16), _NT, preferred_element_type=F32)
            st = st * jnp.exp(lasts[j][:, ks]) + lax.dot_general(
                v[rows, vs], kd[rows, ks], _TN, preferred_element_type=F32)
            on = o * lax.rsqrt(jnp.mean(o * o, axis=-1, keepdims=True) + EPS) * gng_ref[...]
            oa_o[rows, vs] = (on * gate[rows, vs]).astype(BF16)
        st_ref[gh] = st

    x = x_ref[...]
    h = x * lax.rsqrt(jnp.mean(x * x, axis=-1, keepdims=True) + EPS) * ng_ref[...]
    hb = h.astype(BF16)

    def proj(w_ref):
        return jnp.dot(hb, w_ref[...], preferred_element_type=F32)

    cur = {}

    half = SB_WIDTH // 2

    def piece_sq(j):
        def run_piece():
            cur["sq", j] = (proj(wv(sq_off + j * half, half))
                            * (SB_HD ** -0.5 * LOG2E)).astype(BF16)
        return run_piece

    def piece_skv(name, off, j):
        def run_piece():
            cur[name, j] = proj(wv(off + j * half, half)).astype(BF16)
        return run_piece

    def piece_gla_in():
        cur["code"] = proj(wgr).astype(BF16)
        cur["q"] = proj(wgq) * (GLA_HK ** -0.5)
        cur["k"] = proj(wgk)

    def piece_decay():
        u = jnp.dot(cur["code"], wdu_ref[...], preferred_element_type=F32) + bdec_ref[...]
        la = -_softplus(-u) * (1.0 / GLA_TAU)
        cur["bcum"] = jnp.dot(bd_tril_b, la.astype(BF16), preferred_element_type=F32)

    def piece_qkd():
        bcum = cur["bcum"]
        cur["v"] = proj(wgv).astype(BF16)
        cur["lasts"] = [bcum[(j + 1) * c - 1:(j + 1) * c, :] for j in range(GLA_GROUP)]
        blast = jnp.concatenate([jnp.broadcast_to(l, (c, GLA_DK)) for l in cur["lasts"]], axis=0)
        cur["qe"] = (cur["q"] * jnp.exp(bcum)).astype(BF16)
        cur["ke"] = (cur["k"] * jnp.exp(-bcum)).astype(BF16)
        cur["kd"] = (cur["k"] * jnp.exp(blast - bcum)).astype(BF16)

    def piece_gate():
        g = proj(wgg)
        cur["gate"] = g * _sigmoid(g)

    def piece_gla(gh):
        return lambda: gla_head(gh, cur["qe"], cur["ke"], cur["kd"], cur["v"], cur["gate"],
                                cur["lasts"])

    def piece_sg(j):
        cols = slice(j * half, (j + 1) * half)
        def run_piece():
            g = proj(wv(sg_off + j * half, half))
            sg_o[:, cols] = (g * _sigmoid(g)).astype(BF16)
        return run_piece

    def piece_m(j):
        cols = slice(j * half, (j + 1) * half)
        def run_piece():
            m_o[:, cols] = _sigmoid(proj(wv(j * half, half)) + bg_ref[:, cols]).astype(BF16)
        return run_piece

    pieces = [piece_gla_in, piece_decay, piece_qkd, piece_gate,
              piece_gla(0), piece_sq(0), piece_gla(1), piece_sq(1),
              piece_gla(2), piece_skv("sk", sk_off, 0), piece_gla(3), piece_skv("sk", sk_off, 1),
              piece_skv("sv", sv_off, 0), piece_skv("sv", sv_off, 1), piece_sg(0), piece_sg(1),
              lambda: (piece_m(0)(), piece_m(1)()), lambda: (piece_m(2)(), piece_m(3)())]

    qt_p = p % tiles_per_seq
    rows_d = pl.ds(pl.multiple_of(qt_p * t, t), t)
    rows_b = pl.ds(pl.multiple_of(jnp.maximum(qt_p - 1, 0) * t, t), t)
    has_prev = (qt_p > 0).astype(F32)
    n_blocks = 2 * SB_HEADS
    blk = [dict() for _ in range(n_blocks)]

    def stage1(n):
        hd, before = divmod(n, 2)
        hs = slice(hd * SB_HD, (hd + 1) * SB_HD)
        rows = rows_b if before else rows_d
        z = lax.dot_general(k_scr[rows, hs], q_scr[:, hs], _NT, preferred_element_type=F32)
        if not before:
            z = z + causal_bias
        sp = jnp.maximum(z, 0.0) + jnp.log(1.0 + jnp.exp2(-jnp.abs(z))) * LOG2E
        blk[n].update(sp=sp.astype(BF16), lb=z - sp, rs=jnp.sum(sp, axis=0, keepdims=True))

    def stage2(n):
        x = blk[n]["lb"] - jnp.dot(tri_u, blk[n]["sp"], preferred_element_type=F32)
        if n % 2:
            x = x - blk[n - 1]["rs"]
        blk[n]["a"] = jnp.exp2(x).astype(BF16)

    def stage3(n):
        hd, before = divmod(n, 2)
        hs = slice(hd * SB_HD, (hd + 1) * SB_HD)
        rows = rows_b if before else rows_d
        blk[n]["pv"] = lax.dot_general(v_scr[rows, hs], blk[n]["a"], _TN,
                                       preferred_element_type=F32)
        if before:
            acc_ref[:, hs] = jnp.transpose(blk[n - 1]["pv"] + blk[n]["pv"] * has_prev)
            run = blk[n - 1]["rs"] + blk[n]["rs"] * has_prev
            run_ref[hd] = jnp.broadcast_to(run, (SUBLANES, t))
            blk[n]["run"] = run

    for n in range(n_blocks + 2):
        if 0 <= n - 2 < n_blocks:
            stage3(n - 2)
        if 0 <= n - 1 < n_blocks:
            stage2(n - 1)
        if n < n_blocks:
            stage1(n)
        if n < len(pieces):
            pieces[n]()
    run_min = blk[1]["run"]
    for hd in range(1, SB_HEADS):
        run_min = jnp.minimum(run_min, blk[2 * hd + 1]["run"])

    sb_rest(p, jnp.min(run_min))

    rows_now = pl.ds(pl.multiple_of((i % tiles_per_seq) * t, t), t)
    for j in range(2):
        cols = slice(j * half, (j + 1) * half)
        q_scr[:, cols] = cur["sq", j]
        k_scr[rows_now, cols] = cur["sk", j]
        v_scr[rows_now, cols] = cur["sv", j]

    @pl.when(i == last)
    def _():
        low = sb_head(0, i)
        for hd in range(1, SB_HEADS):
            low = jnp.minimum(low, sb_head(hd, i))
        sb_rest(i, jnp.min(low))


_REF_HEAD = 2 * GLA_DK + 2 * GLA_DV
_REF_TAIL = _REF_HEAD + GLA_RANK
_REF_M = _REF_TAIL + 4 * SB_WIDTH
_REF_COLS = _REF_M + 2 * D_MODEL
_PK_HEAD = 2 * D_MODEL
_PK_TAIL = _PK_HEAD + _REF_HEAD
_PK_RANK = _PK_TAIL + 4 * SB_WIDTH
PACK_STEPS = _PK_RANK // PACK_COLS + 1


def _pack_step(j, wt_ref, w_scr):
    w = wt_ref[...]
    n_row = lax.broadcasted_iota(jnp.int32, w.shape, 0)
    is_code = j == PACK_STEPS - 1
    w = jnp.where(is_code & (n_row >= GLA_RANK), 0.0, w).astype(BF16)
    r = lax.broadcasted_iota(jnp.int32, (MXU_COLS, MXU_COLS), 0)
    cidx = lax.broadcasted_iota(jnp.int32, (MXU_COLS, MXU_COLS), 1)
    eye = jnp.where(r == cidx, 1.0, 0.0).astype(BF16)
    for nn in range(0, w.shape[0], MXU_COLS):
        for kk in range(0, w.shape[1], MXU_COLS):
            w_scr[j, kk:kk + MXU_COLS, nn:nn + MXU_COLS] = lax.dot_general(
                eye, w[nn:nn + MXU_COLS, kk:kk + MXU_COLS], _NT,
                preferred_element_type=F32).astype(BF16)


def _mixer_kernel(src_ref, wt_ref, x_ref, ng_ref, bg_ref, wdu_ref, bdec_ref, gng_ref,
                  oa_o, ob_o, sg_o, m_o,
                  w_scr, st_ref, q_scr, k_scr, v_scr, acc_ref, run_ref, *, tiles_per_seq):
    del src_ref
    s = pl.program_id(0)

    @pl.when(s < PACK_STEPS)
    def _():
        _pack_step(s, wt_ref, w_scr)

    @pl.when(s >= PACK_STEPS)
    def _():
        _mixer_step(s - PACK_STEPS, pl.num_programs(0) - 1 - PACK_STEPS,
                    x_ref, ng_ref, w_scr, bg_ref, wdu_ref, bdec_ref, gng_ref,
                    oa_o, ob_o, sg_o, m_o,
                    st_ref, q_scr, k_scr, v_scr, acc_ref, run_ref, tiles_per_seq)


def _mixers(x2, norm_g, w_in, b_gate, wdu, b_dec, gla_norm_g, seq):
    m = x2.shape[0]
    d = w_in.shape[0]
    tm = GLA_GROUP * GLA_CHUNK
    src = ([_REF_M + c for c in range(0, _PK_HEAD, PACK_COLS)]
           + [c for c in range(0, _REF_HEAD, PACK_COLS)]
           + [_REF_TAIL + c for c in range(0, _REF_M - _REF_TAIL, PACK_COLS)]
           + [_REF_HEAD])
    assert len(src) == PACK_STEPS and _REF_HEAD + PACK_COLS <= _REF_COLS
    assert all(c % GLA_RANK == 0 for c in src)
    src = [c // GLA_RANK for c in src]
    const = lambda s, src_ref: (0, 0)
    tile_of = lambda s: jnp.maximum(s - PACK_STEPS, 0)
    in_specs = [pl.BlockSpec((pl.Element(PACK_COLS), pl.Element(d)),
                             lambda s, src_ref: (
                                 src_ref[jnp.minimum(s, PACK_STEPS - 1)] * GLA_RANK, 0)),
                pl.BlockSpec((tm, D_MODEL), lambda s, src_ref: (tile_of(s), 0)),
                pl.BlockSpec((1, D_MODEL), const),
                pl.BlockSpec((1, 2 * D_MODEL), const),
                pl.BlockSpec((RANK_PAD, GLA_DK), const),
                pl.BlockSpec((1, GLA_DK), const),
                pl.BlockSpec((1, GLA_HV), const)]
    out_widths = [GLA_DV, SB_WIDTH, SB_WIDTH, 2 * D_MODEL]
    out_specs = [pl.BlockSpec((tm, n), lambda s, src_ref: (tile_of(s), 0)) for n in out_widths]
    out_specs[1] = pl.BlockSpec(
        (2 * tm, SB_WIDTH), lambda s, src_ref: (jnp.maximum(tile_of(s) - 1, 0) // 2, 0))
    out_shape = [jax.ShapeDtypeStruct((m, n), BF16) for n in out_widths]
    grid_spec = pltpu.PrefetchScalarGridSpec(
        num_scalar_prefetch=1,
        grid=(PACK_STEPS + m // tm,),
        in_specs=in_specs,
        out_specs=out_specs,
        scratch_shapes=[
            pltpu.VMEM((PACK_STEPS, D_MODEL, PACK_COLS), BF16),
            pltpu.VMEM((GLA_HEADS, GLA_HV, GLA_HK), F32),
            pltpu.VMEM((tm, SB_WIDTH), BF16),
            pltpu.VMEM((seq, SB_WIDTH), BF16),
            pltpu.VMEM((seq, SB_WIDTH), BF16),
            pltpu.VMEM((tm, SB_WIDTH), F32),
            pltpu.VMEM((SB_HEADS, SUBLANES, tm), F32),
        ])
    return pl.pallas_call(
        functools.partial(_mixer_kernel, tiles_per_seq=seq // tm),
        grid_spec=grid_spec,
        out_shape=out_shape,
        compiler_params=pltpu.CompilerParams(
            dimension_semantics=("arbitrary",), vmem_limit_bytes=VMEM_LIMIT),
        name="mixers",
    )(jnp.asarray(src, jnp.int32), w_in.T, x2, norm_g, b_gate, wdu, b_dec, gla_norm_g)


def _out_kernel(oa_ref, ob_ref, sg_ref, m_ref, x_ref, wpa_ref, wpb_ref, wo_ref, fg_ref, o_ref,
                wpa_b, wpb_b, wo_b):
    @pl.when(pl.program_id(0) == 0)
    def _():
        wpa_b[...] = wpa_ref[...].astype(BF16)
        wpb_b[...] = wpb_ref[...].astype(BF16)
        wo_b[...] = wo_ref[...].astype(BF16)

    tm = o_ref.shape[0]
    groups = [pl.ds(r, OUT_ROWS) for r in range(0, tm, OUT_ROWS)]

    def branches(rows):
        ya = jnp.dot(oa_ref[rows, :], wpa_b[...], preferred_element_type=F32)
        yb = jnp.dot(ob_ref[rows, :] * sg_ref[rows, :], wpb_b[...], preferred_element_type=F32)
        return ya, yb

    def merge(rows, ya, yb):
        merged = (m_ref[rows, :D_MODEL].astype(F32) * ya + m_ref[rows, D_MODEL:].astype(F32) * yb)
        return x_ref[rows, :] + jnp.dot(merged.astype(BF16), wo_b[...],
                                        preferred_element_type=F32)

    def finish(rows, y):
        o_ref[rows, :] = y * lax.rsqrt(jnp.mean(y * y, axis=-1, keepdims=True) + EPS) * fg_ref[...]

    yab = branches(groups[0])
    y_prev = None
    for g, rows in enumerate(groups):
        yab_next = branches(groups[g + 1]) if g + 1 < len(groups) else None
        y = merge(rows, *yab)
        if y_prev is not None:
            finish(groups[g - 1], y_prev)
        yab, y_prev = yab_next, y
    finish(groups[-1], y_prev)


def _out(oa, ob, sg, gates, x2, wpa, wpb, wo, final_g, tm):
    m = x2.shape[0]
    const = lambda i: (0, 0)
    rows = lambda n: pl.BlockSpec((tm, n), lambda i: (i, 0))
    wspec = pl.BlockSpec((D_MODEL, D_MODEL), const, pipeline_mode=pl.Buffered(1))
    return pl.pallas_call(
        _out_kernel,
        grid=(m // tm,),
        in_specs=[rows(GLA_DV), rows(SB_WIDTH), rows(SB_WIDTH), rows(2 * D_MODEL), rows(D_MODEL),
                  wspec, wspec, wspec, pl.BlockSpec((1, D_MODEL), const)],
        out_specs=rows(D_MODEL),
        out_shape=jax.ShapeDtypeStruct((m, D_MODEL), F32),
        scratch_shapes=[pltpu.VMEM((D_MODEL, D_MODEL), BF16)] * 3,
        compiler_params=pltpu.CompilerParams(
            dimension_semantics=("arbitrary",), vmem_limit_bytes=VMEM_LIMIT),
        name="outproj",
    )(oa, ob, sg, gates, x2, wpa, wpb, wo, final_g)


def kernel(x, norm_g, w_in, w_dec_up, b_dec, gla_norm_g, w_pa, w_pb, b_gate, w_o, final_g):
    batch, seq, d = x.shape
    tile, out_rows = GLA_GROUP * GLA_CHUNK, 512
    assert d == D_MODEL and w_in.shape == (D_MODEL, _REF_COLS)
    assert seq % tile == 0 and (batch * seq) % out_rows == 0
    assert (batch * seq // tile) % 2 == 0
    x2 = x.reshape(batch * seq, D_MODEL)

    wdu = jnp.pad(w_dec_up, ((0, RANK_PAD - GLA_RANK), (0, 0))).astype(BF16)

    oa, ob, sg, gates = _mixers(
        x2, norm_g.reshape(1, D_MODEL), w_in, b_gate.reshape(1, 2 * D_MODEL),
        wdu, b_dec.reshape(1, GLA_DK), gla_norm_g.reshape(1, GLA_HV), seq)

    out = _out(oa, ob, sg, gates, x2, w_pa, w_pb, w_o, final_g.reshape(1, D_MODEL), tm=out_rows)
    return out.reshape(batch, seq, D_MODEL)
```

```python
import functools

import jax
import jax.numpy as jnp
from jax import lax
from jax.experimental import pallas as pl
from jax.experimental.pallas import tpu as pltpu

D_MODEL = 1024
GLA_HEADS = 4
GLA_HK = 128
GLA_HV = 256
GLA_DK = GLA_HEADS * GLA_HK
GLA_DV = GLA_HEADS * GLA_HV
GLA_RANK = 16
GLA_TAU = 16.0
GLA_CHUNK = 64
SB_HEADS = 8
SB_HD = 128
SB_WIDTH = SB_HEADS * SB_HD
EPS = 1e-6
LOG2E = 1.4426950408889634
MASK_BIAS = -1e30
SB_ZERO_BITS = 160.0

GLA_GROUP = 4

SUBLANES = 8
MXU_COLS = 256
RANK_PAD = MXU_COLS
PACK_COLS = 1024
PACK_SPLIT = 2
OUT_ROWS = 256
VMEM_LIMIT = 56 * 1024 * 1024

F32 = jnp.float32
BF16 = jnp.bfloat16

_NT = (((1,), (1,)), ((), ()))
_TN = (((0,), (0,)), ((), ()))


def _sigmoid(x):
    return 1.0 / (1.0 + jnp.exp(-x))


def _softplus(x):
    return jnp.maximum(x, 0.0) + jnp.log(1.0 + jnp.exp(-jnp.abs(x)))


def _sb_block(q, k, v, bias, carry, tri_u):
    z = lax.dot_general(k, q, _NT, preferred_element_type=F32)
    if bias is not None:
        z = z + bias
    sp = jnp.maximum(z, 0.0) + jnp.log(1.0 + jnp.exp2(-jnp.abs(z))) * LOG2E
    x = (z - sp) - jnp.dot(tri_u, sp.astype(BF16), preferred_element_type=F32)
    if carry is not None:
        x = x - carry
    pv = lax.dot_general(v, jnp.exp2(x).astype(BF16), _TN, preferred_element_type=F32)
    return pv, jnp.sum(sp, axis=0, keepdims=True)


def _mixer_step(i, last, x_ref, ng_ref, w_scr, bg_ref, wdu_ref, bdec_ref, gng_ref,
                oa_o, ob_o, sg_o, m_o,
                st_ref, q_scr, k_scr, v_scr, acc_ref, run_ref, tiles_per_seq):
    c = GLA_CHUNK
    t = GLA_GROUP * c
    p = jnp.maximum(i - 1, 0)

    def wv(off, width):
        lo = off % PACK_COLS
        return w_scr.at[off // PACK_COLS, :, lo:lo + width]

    wgq, wgk = wv(_PK_HEAD, GLA_DK), wv(_PK_HEAD + GLA_DK, GLA_DK)
    wgv, wgg = wv(_PK_HEAD + 2 * GLA_DK, GLA_DV), wv(_PK_HEAD + 2 * GLA_DK + GLA_DV, GLA_DV)
    wgr = wv(_PK_RANK, RANK_PAD)
    sq_off, sk_off, sv_off, sg_off = (_PK_TAIL + n * SB_WIDTH for n in range(4))

    @pl.when(i == 0)
    def _():
        q_scr[...] = jnp.zeros((t, SB_WIDTH), BF16)
        k_scr[pl.ds(0, t), :] = jnp.zeros((t, SB_WIDTH), BF16)
        v_scr[pl.ds(0, t), :] = jnp.zeros((t, SB_WIDTH), BF16)

    @pl.when(i % tiles_per_seq == 0)
    def _():
        st_ref[...] = jnp.zeros_like(st_ref)

    row = lax.broadcasted_iota(jnp.int32, (t, t), 0)
    col = lax.broadcasted_iota(jnp.int32, (t, t), 1)
    tri_u = jnp.where(col > row, 1.0, 0.0).astype(BF16)
    causal_bias = jnp.where(row < col, 0.0, MASK_BIAS)
    bd_tril = (row >= col) & ((row ^ col) < c)
    bd_tril_b = jnp.where(bd_tril, 1.0, 0.0).astype(BF16)

    def sb_head(hd, tile):
        qt = tile % tiles_per_seq
        rows_d = pl.ds(pl.multiple_of(qt * t, t), t)
        rows_p = pl.ds(pl.multiple_of(jnp.maximum(qt - 1, 0) * t, t), t)
        has_prev = (qt > 0).astype(F32)
        hs = slice(hd * SB_HD, (hd + 1) * SB_HD)
        qh = q_scr[:, hs]
        pv_d, rs_d = _sb_block(qh, k_scr[rows_d, hs], v_scr[rows_d, hs], causal_bias, None, tri_u)
        pv_p, rs_p = _sb_block(qh, k_scr[rows_p, hs], v_scr[rows_p, hs], None, rs_d, tri_u)
        acc_ref[:, hs] = jnp.transpose(pv_d + pv_p * has_prev)
        run = rs_d + rs_p * has_prev
        run_ref[hd] = jnp.broadcast_to(run, (SUBLANES, t))
        return run

    def sb_rest(tile, lowest):
        qt = tile % tiles_per_seq

        def more(state):
            kb, low = state
            return (kb >= 0) & (low <= SB_ZERO_BITS)

        def walk(state):
            kb, _ = state
            keys = pl.ds(pl.multiple_of(kb * t, t), t)
            tri_w = jnp.where(col > row, 1.0, 0.0).astype(BF16)
            low = None
            for hd in range(SB_HEADS):
                hs = slice(hd * SB_HD, (hd + 1) * SB_HD)
                carry = run_ref[hd][0:1, :]
                pv, rs = _sb_block(q_scr[:, hs], k_scr[keys, hs], v_scr[keys, hs],
                                   None, carry, tri_w)
                acc_ref[:, hs] += jnp.transpose(pv)
                run = carry + rs
                run_ref[hd] = jnp.broadcast_to(run, (SUBLANES, t))
                low = run if low is None else jnp.minimum(low, run)
            return kb - 1, jnp.min(low)

        lax.while_loop(more, walk, (qt - 2, lowest))
        ob_o[pl.ds(pl.multiple_of((tile % 2) * t, t), t), :] = acc_ref[...].astype(BF16)

    def gla_head(gh, qe, ke, kd, v, gate, lasts):
        ks = slice(gh * GLA_HK, (gh + 1) * GLA_HK)
        vs = slice(gh * GLA_HV, (gh + 1) * GLA_HV)
        attn = lax.dot_general(qe[:, ks], ke[:, ks], _NT, preferred_element_type=F32)
        attn = jnp.where(bd_tril, attn, 0.0).astype(BF16)
        intra = jnp.dot(attn, v[:, vs], preferred_element_type=F32)
        st = st_ref[gh]
        for j in range(GLA_GROUP):
            rows = slice(j * c, (j + 1) * c)
            o = intra[rows] + lax.dot_general(
                qe[rows, ks], st.astype(BF16), _NT, preferred_element_type=F32)
            st = st * jnp.exp(lasts[j][:, ks]) + lax.dot_general(
                v[rows, vs], kd[rows, ks], _TN, preferred_element_type=F32)
            on = o * lax.rsqrt(jnp.mean(o * o, axis=-1, keepdims=True) + EPS) * gng_ref[...]
            oa_o[rows, vs] = (on * gate[rows, vs]).astype(BF16)
        st_ref[gh] = st

    x = x_ref[...]
    h = x * lax.rsqrt(jnp.mean(x * x, axis=-1, keepdims=True) + EPS) * ng_ref[...]
    hb = h.astype(BF16)

    def proj(w_ref):
        return jnp.dot(hb, w_ref[...], preferred_element_type=F32)

    cur = {}

    half = SB_WIDTH // 2

    def piece_sq(j):
        def run_piece():
            cur["sq", j] = (proj(wv(sq_off + j * half, half))
                            * (SB_HD ** -0.5 * LOG2E)).astype(BF16)
        return run_piece

    def piece_skv(name, off, j):
        def run_piece():
            cur[name, j] = proj(wv(off + j * half, half)).astype(BF16)
        return run_piece

    def piece_gla_in():
        cur["code"] = proj(wgr).astype(BF16)
        cur["q"] = proj(wgq) * (GLA_HK ** -0.5)
        cur["k"] = proj(wgk)

    def piece_decay():
        u = jnp.dot(cur["code"], wdu_ref[...], preferred_element_type=F32) + bdec_ref[...]
        la = -_softplus(-u) * (1.0 / GLA_TAU)
        cur["bcum"] = jnp.dot(bd_tril_b, la.astype(BF16), preferred_element_type=F32)

    def piece_qkd():
        bcum = cur["bcum"]
        cur["v"] = proj(wgv).astype(BF16)
        cur["lasts"] = [bcum[(j + 1) * c - 1:(j + 1) * c, :] for j in range(GLA_GROUP)]
        blast = jnp.concatenate([jnp.broadcast_to(l, (c, GLA_DK)) for l in cur["lasts"]], axis=0)
        cur["qe"] = (cur["q"] * jnp.exp(bcum)).astype(BF16)
        cur["ke"] = (cur["k"] * jnp.exp(-bcum)).astype(BF16)
        cur["kd"] = (cur["k"] * jnp.exp(blast - bcum)).astype(BF16)

    def piece_gate():
        g = proj(wgg)
        cur["gate"] = g * _sigmoid(g)

    def piece_gla(gh):
        return lambda: gla_head(gh, cur["qe"], cur["ke"], cur["kd"], cur["v"], cur["gate"],
                                cur["lasts"])

    def piece_sg(j):
        cols = slice(j * half, (j + 1) * half)
        def run_piece():
            g = proj(wv(sg_off + j * half, half))
            sg_o[:, cols] = (g * _sigmoid(g)).astype(BF16)
        return run_piece

    def piece_m(j):
        cols = slice(j * half, (j + 1) * half)
        def run_piece():
            m_o[:, cols] = _sigmoid(proj(wv(j * half, half)) + bg_ref[:, cols]).astype(BF16)
        return run_piece

    pieces = [piece_gla_in, piece_decay, piece_qkd, piece_gate,
              piece_gla(0), piece_sq(0), piece_gla(1), piece_sq(1),
              piece_gla(2), piece_skv("sk", sk_off, 0), piece_gla(3), piece_skv("sk", sk_off, 1),
              piece_skv("sv", sv_off, 0), piece_skv("sv", sv_off, 1), piece_sg(0), piece_sg(1),
              lambda: (piece_m(0)(), piece_m(1)()), lambda: (piece_m(2)(), piece_m(3)())]

    qt_p = p % tiles_per_seq
    rows_d = pl.ds(pl.multiple_of(qt_p * t, t), t)
    rows_b = pl.ds(pl.multiple_of(jnp.maximum(qt_p - 1, 0) * t, t), t)
    has_prev = (qt_p > 0).astype(F32)
    n_blocks = 2 * SB_HEADS
    blk = [dict() for _ in range(n_blocks)]

    def stage1(n):
        hd, before = divmod(n, 2)
        hs = slice(hd * SB_HD, (hd + 1) * SB_HD)
        rows = rows_b if before else rows_d
        z = lax.dot_general(k_scr[rows, hs], q_scr[:, hs], _NT, preferred_element_type=F32)
        if not before:
            z = z + causal_bias
        sp = jnp.maximum(z, 0.0) + jnp.log(1.0 + jnp.exp2(-jnp.abs(z))) * LOG2E
        blk[n].update(sp=sp.astype(BF16), lb=z - sp, rs=jnp.sum(sp, axis=0, keepdims=True))

    def stage2(n):
        x = blk[n]["lb"] - jnp.dot(tri_u, blk[n]["sp"], preferred_element_type=F32)
        if n % 2:
            x = x - blk[n - 1]["rs"]
        blk[n]["a"] = jnp.exp2(x).astype(BF16)

    def stage3(n):
        hd, before = divmod(n, 2)
        hs = slice(hd * SB_HD, (hd + 1) * SB_HD)
        rows = rows_b if before else rows_d
        blk[n]["pv"] = lax.dot_general(v_scr[rows, hs], blk[n]["a"], _TN,
                                       preferred_element_type=F32)
        if before:
            acc_ref[:, hs] = jnp.transpose(blk[n - 1]["pv"] + blk[n]["pv"] * has_prev)
            run = blk[n - 1]["rs"] + blk[n]["rs"] * has_prev
            run_ref[hd] = jnp.broadcast_to(run, (SUBLANES, t))
            blk[n]["run"] = run

    for n in range(n_blocks + 2):
        if 0 <= n - 2 < n_blocks:
            stage3(n - 2)
        if 0 <= n - 1 < n_blocks:
            stage2(n - 1)
        if n < n_blocks:
            stage1(n)
        if n < len(pieces):
            pieces[n]()
    run_min = blk[1]["run"]
    for hd in range(1, SB_HEADS):
        run_min = jnp.minimum(run_min, blk[2 * hd + 1]["run"])

    sb_rest(p, jnp.min(run_min))

    rows_now = pl.ds(pl.multiple_of((i % tiles_per_seq) * t, t), t)
    for j in range(2):
        cols = slice(j * half, (j + 1) * half)
        q_scr[:, cols] = cur["sq", j]
        k_scr[rows_now, cols] = cur["sk", j]
        v_scr[rows_now, cols] = cur["sv", j]

    @pl.when(i == last)
    def _():
        low = sb_head(0, i)
        for hd in range(1, SB_HEADS):
            low = jnp.minimum(low, sb_head(hd, i))
        sb_rest(i, jnp.min(low))


_REF_HEAD = 2 * GLA_DK + 2 * GLA_DV
_REF_TAIL = _REF_HEAD + GLA_RANK
_REF_M = _REF_TAIL + 4 * SB_WIDTH
_REF_COLS = _REF_M + 2 * D_MODEL
_PK_HEAD = 2 * D_MODEL
_PK_TAIL = _PK_HEAD + _REF_HEAD
_PK_RANK = _PK_TAIL + 4 * SB_WIDTH
PACK_STEPS = _PK_RANK // PACK_COLS + 1


def _pack_step(j, wt_refs, w_scr):
    is_code = j == PACK_STEPS - 1
    r = lax.broadcasted_iota(jnp.int32, (MXU_COLS, MXU_COLS), 0)
    cidx = lax.broadcasted_iota(jnp.int32, (MXU_COLS, MXU_COLS), 1)
    eye = jnp.where(r == cidx, 1.0, 0.0).astype(BF16)
    for part, wt_ref in enumerate(wt_refs):
        w = wt_ref[...]
        col0 = part * w.shape[0]
        n_row = lax.broadcasted_iota(jnp.int32, w.shape, 0) + col0
        w = jnp.where(is_code & (n_row >= GLA_RANK), 0.0, w).astype(BF16)
        for nn in range(0, w.shape[0], MXU_COLS):
            for kk in range(0, w.shape[1], MXU_COLS):
                w_scr[j, kk:kk + MXU_COLS, col0 + nn:col0 + nn + MXU_COLS] = lax.dot_general(
                    eye, w[nn:nn + MXU_COLS, kk:kk + MXU_COLS], _NT,
                    preferred_element_type=F32).astype(BF16)


def _mixer_kernel(src_ref, wta_ref, wtb_ref, x_ref, ng_ref, bg_ref, wdu_ref, bdec_ref, gng_ref,
                  oa_o, ob_o, sg_o, m_o,
                  w_scr, st_ref, q_scr, k_scr, v_scr, acc_ref, run_ref, *, tiles_per_seq):
    del src_ref
    s = pl.program_id(0)

    @pl.when(s < PACK_STEPS)
    def _():
        _pack_step(s, (wta_ref, wtb_ref), w_scr)

    @pl.when(s >= PACK_STEPS)
    def _():
        _mixer_step(s - PACK_STEPS, pl.num_programs(0) - 1 - PACK_STEPS,
                    x_ref, ng_ref, w_scr, bg_ref, wdu_ref, bdec_ref, gng_ref,
                    oa_o, ob_o, sg_o, m_o,
                    st_ref, q_scr, k_scr, v_scr, acc_ref, run_ref, tiles_per_seq)


def _mixers(x2, norm_g, w_in, b_gate, wdu, b_dec, gla_norm_g, seq):
    m = x2.shape[0]
    d = w_in.shape[0]
    tm = GLA_GROUP * GLA_CHUNK
    src = ([_REF_M + c for c in range(0, _PK_HEAD, PACK_COLS)]
           + [c for c in range(0, _REF_HEAD, PACK_COLS)]
           + [_REF_TAIL + c for c in range(0, _REF_M - _REF_TAIL, PACK_COLS)]
           + [_REF_HEAD])
    assert len(src) == PACK_STEPS and _REF_HEAD + PACK_COLS <= _REF_COLS
    assert all(c % GLA_RANK == 0 for c in src)
    src = [c // GLA_RANK for c in src]
    const = lambda s, src_ref: (0, 0)
    tile_of = lambda s: jnp.maximum(s - PACK_STEPS, 0)
    part = PACK_COLS // PACK_SPLIT
    window = lambda k: pl.BlockSpec(
        (pl.Element(part), pl.Element(d)),
        lambda s, src_ref: ((src_ref[jnp.minimum(s, PACK_STEPS - 1)] + k * (part // GLA_RANK))
                            * GLA_RANK, 0))
    in_specs = [window(0), window(1),
                pl.BlockSpec((tm, D_MODEL), lambda s, src_ref: (tile_of(s), 0)),
                pl.BlockSpec((1, D_MODEL), const),
                pl.BlockSpec((1, 2 * D_MODEL), const),
                pl.BlockSpec((RANK_PAD, GLA_DK), const),
                pl.BlockSpec((1, GLA_DK), const),
                pl.BlockSpec((1, GLA_HV), const)]
    out_widths = [GLA_DV, SB_WIDTH, SB_WIDTH, 2 * D_MODEL]
    out_specs = [pl.BlockSpec((tm, n), lambda s, src_ref: (tile_of(s), 0)) for n in out_widths]
    out_specs[1] = pl.BlockSpec(
        (2 * tm, SB_WIDTH), lambda s, src_ref: (jnp.maximum(tile_of(s) - 1, 0) // 2, 0))
    out_shape = [jax.ShapeDtypeStruct((m, n), BF16) for n in out_widths]
    grid_spec = pltpu.PrefetchScalarGridSpec(
        num_scalar_prefetch=1,
        grid=(PACK_STEPS + m // tm,),
        in_specs=in_specs,
        out_specs=out_specs,
        scratch_shapes=[
            pltpu.VMEM((PACK_STEPS, D_MODEL, PACK_COLS), BF16),
            pltpu.VMEM((GLA_HEADS, GLA_HV, GLA_HK), F32),
            pltpu.VMEM((tm, SB_WIDTH), BF16),
            pltpu.VMEM((seq, SB_WIDTH), BF16),
            pltpu.VMEM((seq, SB_WIDTH), BF16),
            pltpu.VMEM((tm, SB_WIDTH), F32),
            pltpu.VMEM((SB_HEADS, SUBLANES, tm), F32),
        ])
    return pl.pallas_call(
        functools.partial(_mixer_kernel, tiles_per_seq=seq // tm),
        grid_spec=grid_spec,
        out_shape=out_shape,
        compiler_params=pltpu.CompilerParams(
            dimension_semantics=("arbitrary",), vmem_limit_bytes=VMEM_LIMIT),
        name="mixers",
    )(jnp.asarray(src, jnp.int32), w_in.T, w_in.T, x2, norm_g, b_gate, wdu, b_dec, gla_norm_g)


def _out_kernel(oa_ref, ob_ref, sg_ref, m_ref, x_ref, wpa_ref, wpb_ref, wo_ref, fg_ref, o_ref,
                wpa_b, wpb_b, wo_b):
    @pl.when(pl.program_id(0) == 0)
    def _():
        wpa_b[...] = wpa_ref[...].astype(BF16)
        wpb_b[...] = wpb_ref[...].astype(BF16)
        wo_b[...] = wo_ref[...].astype(BF16)

    tm = o_ref.shape[0]
    groups = [pl.ds(r, OUT_ROWS) for r in range(0, tm, OUT_ROWS)]

    def branches(rows):
        ya = jnp.dot(oa_ref[rows, :], wpa_b[...], preferred_element_type=F32)
        yb = jnp.dot(ob_ref[rows, :] * sg_ref[rows, :], wpb_b[...], preferred_element_type=F32)
        return ya, yb

    def merge(rows, ya, yb):
        merged = (m_ref[rows, :D_MODEL].astype(F32) * ya + m_ref[rows, D_MODEL:].astype(F32) * yb)
        return x_ref[rows, :] + jnp.dot(merged.astype(BF16), wo_b[...],
                                        preferred_element_type=F32)

    def finish(rows, y):
        o_ref[rows, :] = y * lax.rsqrt(jnp.mean(y * y, axis=-1, keepdims=True) + EPS) * fg_ref[...]

    yab = branches(groups[0])
    y_prev = None
    for g, rows in enumerate(groups):
        yab_next = branches(groups[g + 1]) if g + 1 < len(groups) else None
        y = merge(rows, *yab)
        if y_prev is not None:
            finish(groups[g - 1], y_prev)
        yab, y_prev = yab_next, y
    finish(groups[-1], y_prev)


def _out(oa, ob, sg, gates, x2, wpa, wpb, wo, final_g, tm):
    m = x2.shape[0]
    const = lambda i: (0, 0)
    rows = lambda n: pl.BlockSpec((tm, n), lambda i: (i, 0))
    wspec = pl.BlockSpec((D_MODEL, D_MODEL), const, pipeline_mode=pl.Buffered(1))
    return pl.pallas_call(
        _out_kernel,
        grid=(m // tm,),
        in_specs=[rows(GLA_DV), rows(SB_WIDTH), rows(SB_WIDTH), rows(2 * D_MODEL), rows(D_MODEL),
                  wspec, wspec, wspec, pl.BlockSpec((1, D_MODEL), const)],
        out_specs=rows(D_MODEL),
        out_shape=jax.ShapeDtypeStruct((m, D_MODEL), F32),
        scratch_shapes=[pltpu.VMEM((D_MODEL, D_MODEL), BF16)] * 3,
        compiler_params=pltpu.CompilerParams(
            dimension_semantics=("arbitrary",), vmem_limit_bytes=VMEM_LIMIT),
        name="outproj",
    )(oa, ob, sg, gates, x2, wpa, wpb, wo, final_g)


def kernel(x, norm_g, w_in, w_dec_up, b_dec, gla_norm_g, w_pa, w_pb, b_gate, w_o, final_g):
    batch, seq, d = x.shape
    tile, out_rows = GLA_GROUP * GLA_CHUNK, 512
    assert d == D_MODEL and w_in.shape == (D_MODEL, _REF_COLS)
    assert seq % tile == 0 and (batch * seq) % out_rows == 0
    assert (batch * seq // tile) % 2 == 0
    x2 = x.reshape(batch * seq, D_MODEL)

    wdu = jnp.pad(w_dec_up, ((0, RANK_PAD - GLA_RANK), (0, 0))).astype(BF16)

    oa, ob, sg, gates = _mixers(
        x2, norm_g.reshape(1, D_MODEL), w_in, b_gate.reshape(1, 2 * D_MODEL),
        wdu, b_dec.reshape(1, GLA_DK), gla_norm_g.reshape(1, GLA_HV), seq)

    out = _out(oa, ob, sg, gates, x2, w_pa, w_pb, w_o, final_g.reshape(1, D_MODEL), tm=out_rows)
    return out.reshape(batch, seq, D_MODEL)
```

```python
import functools

import jax
import jax.numpy as jnp
from jax import lax
from jax.experimental import pallas as pl
from jax.experimental.pallas import tpu as pltpu

D_MODEL = 1024
GLA_HEADS = 4
GLA_HK = 128
GLA_HV = 256
GLA_DK = GLA_HEADS * GLA_HK
GLA_DV = GLA_HEADS * GLA_HV
GLA_RANK = 16
GLA_TAU = 16.0
GLA_CHUNK = 64
SB_HEADS = 8
SB_HD = 128
SB_WIDTH = SB_HEADS * SB_HD
EPS = 1e-6
LOG2E = 1.4426950408889634
MASK_BIAS = -1e30
SB_ZERO_BITS = 160.0

GLA_GROUP = 4

SUBLANES = 8
MXU_COLS = 256
RANK_PAD = MXU_COLS
PACK_COLS = 1024
OUT_ROWS = 256
VMEM_LIMIT = 56 * 1024 * 1024

OUT_VMEM_LIMIT = 63 * 1024 * 1024

F32 = jnp.float32
BF16 = jnp.bfloat16

_NT = (((1,), (1,)), ((), ()))
_TN = (((0,), (0,)), ((), ()))


def _sigmoid(x):
    return 1.0 / (1.0 + jnp.exp(-x))


def _softplus(x):
    return jnp.maximum(x, 0.0) + jnp.log(1.0 + jnp.exp(-jnp.abs(x)))


def _sb_block(q, k, v, bias, carry, tri_u):
    z = lax.dot_general(k, q, _NT, preferred_element_type=F32)
    if bias is not None:
        z = z + bias
    sp = jnp.maximum(z, 0.0) + jnp.log(1.0 + jnp.exp2(-jnp.abs(z))) * LOG2E
    x = (z - sp) - jnp.dot(tri_u, sp.astype(BF16), preferred_element_type=F32)
    if carry is not None:
        x = x - carry
    pv = lax.dot_general(v, jnp.exp2(x).astype(BF16), _TN, preferred_element_type=F32)
    return pv, jnp.sum(sp, axis=0, keepdims=True)


def _mixer_step(i, last, x_ref, ng_ref, w_scr, bg_ref, wdu_ref, bdec_ref, gng_ref,
                oa_o, ob_o, sg_o, m_o,
                st_ref, q_scr, k_scr, v_scr, acc_ref, run_ref, tiles_per_seq):
    c = GLA_CHUNK
    t = GLA_GROUP * c
    p = jnp.maximum(i - 1, 0)

    def wv(off, width):
        lo = off % PACK_COLS
        return w_scr.at[off // PACK_COLS, :, lo:lo + width]

    wgq, wgk = wv(_PK_HEAD, GLA_DK), wv(_PK_HEAD + GLA_DK, GLA_DK)
    wgv, wgg = wv(_PK_HEAD + 2 * GLA_DK, GLA_DV), wv(_PK_HEAD + 2 * GLA_DK + GLA_DV, GLA_DV)
    wgr = wv(_PK_RANK, RANK_PAD)
    sq_off, sk_off, sv_off, sg_off = (_PK_TAIL + n * SB_WIDTH for n in range(4))

    @pl.when(i == 0)
    def _():
        q_scr[...] = jnp.zeros((t, SB_WIDTH), BF16)
        k_scr[pl.ds(0, t), :] = jnp.zeros((t, SB_WIDTH), BF16)
        v_scr[pl.ds(0, t), :] = jnp.zeros((t, SB_WIDTH), BF16)

    @pl.when(i % tiles_per_seq == 0)
    def _():
        st_ref[...] = jnp.zeros_like(st_ref)

    row = lax.broadcasted_iota(jnp.int32, (t, t), 0)
    col = lax.broadcasted_iota(jnp.int32, (t, t), 1)
    tri_u = jnp.where(col > row, 1.0, 0.0).astype(BF16)
    causal_bias = jnp.where(row < col, 0.0, MASK_BIAS)
    bd_tril = (row >= col) & ((row ^ col) < c)
    bd_tril_b = jnp.where(bd_tril, 1.0, 0.0).astype(BF16)

    def sb_head(hd, tile):
        qt = tile % tiles_per_seq
        rows_d = pl.ds(pl.multiple_of(qt * t, t), t)
        rows_p = pl.ds(pl.multiple_of(jnp.maximum(qt - 1, 0) * t, t), t)
        has_prev = (qt > 0).astype(F32)
        hs = slice(hd * SB_HD, (hd + 1) * SB_HD)
        qh = q_scr[:, hs]
        pv_d, rs_d = _sb_block(qh, k_scr[rows_d, hs], v_scr[rows_d, hs], causal_bias, None, tri_u)
        pv_p, rs_p = _sb_block(qh, k_scr[rows_p, hs], v_scr[rows_p, hs], None, rs_d, tri_u)
        acc_ref[:, hs] = jnp.transpose(pv_d + pv_p * has_prev)
        run = rs_d + rs_p * has_prev
        run_ref[hd] = jnp.broadcast_to(run, (SUBLANES, t))
        return run

    def sb_rest(tile, lowest):
        qt = tile % tiles_per_seq

        def more(state):
            kb, low = state
            return (kb >= 0) & (low <= SB_ZERO_BITS)

        def walk(state):
            kb, _ = state
            keys = pl.ds(pl.multiple_of(kb * t, t), t)
            tri_w = jnp.where(col > row, 1.0, 0.0).astype(BF16)
            low = None
            for hd in range(SB_HEADS):
                hs = slice(hd * SB_HD, (hd + 1) * SB_HD)
                carry = run_ref[hd][0:1, :]
                pv, rs = _sb_block(q_scr[:, hs], k_scr[keys, hs], v_scr[keys, hs],
                                   None, carry, tri_w)
                acc_ref[:, hs] += jnp.transpose(pv)
                run = carry + rs
                run_ref[hd] = jnp.broadcast_to(run, (SUBLANES, t))
                low = run if low is None else jnp.minimum(low, run)
            return kb - 1, jnp.min(low)

        lax.while_loop(more, walk, (qt - 2, lowest))
        ob_o[pl.ds(pl.multiple_of((tile % 2) * t, t), t), :] = acc_ref[...].astype(BF16)

    def gla_head(gh, qe, ke, kd, v, gate, lasts):
        ks = slice(gh * GLA_HK, (gh + 1) * GLA_HK)
        vs = slice(gh * GLA_HV, (gh + 1) * GLA_HV)
        attn = lax.dot_general(qe[:, ks], ke[:, ks], _NT, preferred_element_type=F32)
        attn = jnp.where(bd_tril, attn, 0.0).astype(BF16)
        intra = jnp.dot(attn, v[:, vs], preferred_element_type=F32)
        st = st_ref[gh]
        for j in range(GLA_GROUP):
            rows = slice(j * c, (j + 1) * c)
            o = intra[rows] + lax.dot_general(
                qe[rows, ks], st.astype(BF16), _NT, preferred_element_type=F32)
            st = st * jnp.exp(lasts[j][:, ks]) + lax.dot_general(
                v[rows, vs], kd[rows, ks], _TN, preferred_element_type=F32)
            on = o * lax.rsqrt(jnp.mean(o * o, axis=-1, keepdims=True) + EPS) * gng_ref[...]
            oa_o[rows, vs] = (on * gate[rows, vs]).astype(BF16)
        st_ref[gh] = st

    x = x_ref[...]
    h = x * lax.rsqrt(jnp.mean(x * x, axis=-1, keepdims=True) + EPS) * ng_ref[...]
    hb = h.astype(BF16)

    def proj(w_ref):
        return jnp.dot(hb, w_ref[...], preferred_element_type=F32)

    cur = {}

    half = SB_WIDTH // 2

    def piece_sq(j):
        def run_piece():
            cur["sq", j] = (proj(wv(sq_off + j * half, half))
                            * (SB_HD ** -0.5 * LOG2E)).astype(BF16)
        return run_piece

    def piece_skv(name, off, j):
        def run_piece():
            cur[name, j] = proj(wv(off + j * half, half)).astype(BF16)
        return run_piece

    def piece_gla_in():
        cur["code"] = proj(wgr).astype(BF16)
        cur["q"] = proj(wgq) * (GLA_HK ** -0.5)
        cur["k"] = proj(wgk)

    def piece_decay():
        u = jnp.dot(cur["code"], wdu_ref[...], preferred_element_type=F32) + bdec_ref[...]
        la = -_softplus(-u) * (1.0 / GLA_TAU)
        cur["bcum"] = jnp.dot(bd_tril_b, la.astype(BF16), preferred_element_type=F32)

    def piece_qkd():
        bcum = cur["bcum"]
        cur["v"] = proj(wgv).astype(BF16)
        cur["lasts"] = [bcum[(j + 1) * c - 1:(j + 1) * c, :] for j in range(GLA_GROUP)]
        blast = jnp.concatenate([jnp.broadcast_to(l, (c, GLA_DK)) for l in cur["lasts"]], axis=0)
        cur["qe"] = (cur["q"] * jnp.exp(bcum)).astype(BF16)
        cur["ke"] = (cur["k"] * jnp.exp(-bcum)).astype(BF16)
        cur["kd"] = (cur["k"] * jnp.exp(blast - bcum)).astype(BF16)

    def piece_gate():
        g = proj(wgg)
        cur["gate"] = g * _sigmoid(g)

    def piece_gla(gh):
        return lambda: gla_head(gh, cur["qe"], cur["ke"], cur["kd"], cur["v"], cur["gate"],
                                cur["lasts"])

    def piece_sg(j):
        cols = slice(j * half, (j + 1) * half)
        def run_piece():
            g = proj(wv(sg_off + j * half, half))
            sg_o[:, cols] = (g * _sigmoid(g)).astype(BF16)
        return run_piece

    def piece_m(j):
        cols = slice(j * half, (j + 1) * half)
        def run_piece():
            m_o[:, cols] = _sigmoid(proj(wv(j * half, half)) + bg_ref[:, cols]).astype(BF16)
        return run_piece

    pieces = [piece_gla_in, piece_decay, piece_qkd, piece_gate,
              piece_gla(0), piece_sq(0), piece_gla(1), piece_sq(1),
              piece_gla(2), piece_skv("sk", sk_off, 0), piece_gla(3), piece_skv("sk", sk_off, 1),
              piece_skv("sv", sv_off, 0), piece_skv("sv", sv_off, 1), piece_sg(0), piece_sg(1),
              lambda: (piece_m(0)(), piece_m(1)()), lambda: (piece_m(2)(), piece_m(3)())]

    qt_p = p % tiles_per_seq
    rows_d = pl.ds(pl.multiple_of(qt_p * t, t), t)
    rows_b = pl.ds(pl.multiple_of(jnp.maximum(qt_p - 1, 0) * t, t), t)
    has_prev = (qt_p > 0).astype(F32)
    n_blocks = 2 * SB_HEADS
    blk = [dict() for _ in range(n_blocks)]

    def stage1(n):
        hd, before = divmod(n, 2)
        hs = slice(hd * SB_HD, (hd + 1) * SB_HD)
        rows = rows_b if before else rows_d
        z = lax.dot_general(k_scr[rows, hs], q_scr[:, hs], _NT, preferred_element_type=F32)
        if not before:
            z = z + causal_bias
        sp = jnp.maximum(z, 0.0) + jnp.log(1.0 + jnp.exp2(-jnp.abs(z))) * LOG2E
        blk[n].update(sp=sp.astype(BF16), lb=z - sp, rs=jnp.sum(sp, axis=0, keepdims=True))

    def stage2(n):
        x = blk[n]["lb"] - jnp.dot(tri_u, blk[n]["sp"], preferred_element_type=F32)
        if n % 2:
            x = x - blk[n - 1]["rs"]
        blk[n]["a"] = jnp.exp2(x).astype(BF16)

    def stage3(n):
        hd, before = divmod(n, 2)
        hs = slice(hd * SB_HD, (hd + 1) * SB_HD)
        rows = rows_b if before else rows_d
        blk[n]["pv"] = lax.dot_general(v_scr[rows, hs], blk[n]["a"], _TN,
                                       preferred_element_type=F32)
        if before:
            acc_ref[:, hs] = jnp.transpose(blk[n - 1]["pv"] + blk[n]["pv"] * has_prev)
            run = blk[n - 1]["rs"] + blk[n]["rs"] * has_prev
            run_ref[hd] = jnp.broadcast_to(run, (SUBLANES, t))
            blk[n]["run"] = run

    for n in range(n_blocks + 2):
        if 0 <= n - 2 < n_blocks:
            stage3(n - 2)
        if 0 <= n - 1 < n_blocks:
            stage2(n - 1)
        if n < n_blocks:
            stage1(n)
        if n < len(pieces):
            pieces[n]()
    run_min = blk[1]["run"]
    for hd in range(1, SB_HEADS):
        run_min = jnp.minimum(run_min, blk[2 * hd + 1]["run"])

    sb_rest(p, jnp.min(run_min))

    rows_now = pl.ds(pl.multiple_of((i % tiles_per_seq) * t, t), t)
    for j in range(2):
        cols = slice(j * half, (j + 1) * half)
        q_scr[:, cols] = cur["sq", j]
        k_scr[rows_now, cols] = cur["sk", j]
        v_scr[rows_now, cols] = cur["sv", j]

    @pl.when(i == last)
    def _():
        low = sb_head(0, i)
        for hd in range(1, SB_HEADS):
            low = jnp.minimum(low, sb_head(hd, i))
        sb_rest(i, jnp.min(low))


_REF_HEAD = 2 * GLA_DK + 2 * GLA_DV
_REF_TAIL = _REF_HEAD + GLA_RANK
_REF_M = _REF_TAIL + 4 * SB_WIDTH
_REF_COLS = _REF_M + 2 * D_MODEL
_PK_HEAD = 2 * D_MODEL
_PK_TAIL = _PK_HEAD + _REF_HEAD
_PK_RANK = _PK_TAIL + 4 * SB_WIDTH
PACK_STEPS = _PK_RANK // PACK_COLS + 1


def _pack_step(j, wt_ref, w_scr):
    w = wt_ref[...]
    n_row = lax.broadcasted_iota(jnp.int32, w.shape, 0)
    is_code = j == PACK_STEPS - 1
    w = jnp.where(is_code & (n_row >= GLA_RANK), 0.0, w).astype(BF16)
    r = lax.broadcasted_iota(jnp.int32, (MXU_COLS, MXU_COLS), 0)
    cidx = lax.broadcasted_iota(jnp.int32, (MXU_COLS, MXU_COLS), 1)
    eye = jnp.where(r == cidx, 1.0, 0.0).astype(BF16)
    for nn in range(0, w.shape[0], MXU_COLS):
        for kk in range(0, w.shape[1], MXU_COLS):
            w_scr[j, kk:kk + MXU_COLS, nn:nn + MXU_COLS] = lax.dot_general(
                eye, w[nn:nn + MXU_COLS, kk:kk + MXU_COLS], _NT,
                preferred_element_type=F32).astype(BF16)


def _mixer_kernel(src_ref, wt_ref, x_ref, ng_ref, bg_ref, wdu_ref, bdec_ref, gng_ref,
                  oa_o, ob_o, sg_o, m_o,
                  w_scr, st_ref, q_scr, k_scr, v_scr, acc_ref, run_ref, *, tiles_per_seq):
    del src_ref
    s = pl.program_id(0)

    @pl.when(s < PACK_STEPS)
    def _():
        _pack_step(s, wt_ref, w_scr)

    @pl.when(s >= PACK_STEPS)
    def _():
        _mixer_step(s - PACK_STEPS, pl.num_programs(0) - 1 - PACK_STEPS,
                    x_ref, ng_ref, w_scr, bg_ref, wdu_ref, bdec_ref, gng_ref,
                    oa_o, ob_o, sg_o, m_o,
                    st_ref, q_scr, k_scr, v_scr, acc_ref, run_ref, tiles_per_seq)


def _mixers(x2, norm_g, w_in, b_gate, wdu, b_dec, gla_norm_g, seq):
    m = x2.shape[0]
    d = w_in.shape[0]
    tm = GLA_GROUP * GLA_CHUNK
    src = ([_REF_M + c for c in range(0, _PK_HEAD, PACK_COLS)]
           + [c for c in range(0, _REF_HEAD, PACK_COLS)]
           + [_REF_TAIL + c for c in range(0, _REF_M - _REF_TAIL, PACK_COLS)]
           + [_REF_HEAD])
    assert len(src) == PACK_STEPS and _REF_HEAD + PACK_COLS <= _REF_COLS
    assert all(c % GLA_RANK == 0 for c in src)
    src = [c // GLA_RANK for c in src]
    const = lambda s, src_ref: (0, 0)
    tile_of = lambda s: jnp.maximum(s - PACK_STEPS, 0)
    in_specs = [pl.BlockSpec((pl.Element(PACK_COLS), pl.Element(d)),
                             lambda s, src_ref: (
                                 src_ref[jnp.minimum(s, PACK_STEPS - 1)] * GLA_RANK, 0)),
                pl.BlockSpec((tm, D_MODEL), lambda s, src_ref: (tile_of(s), 0)),
                pl.BlockSpec((1, D_MODEL), const),
                pl.BlockSpec((1, 2 * D_MODEL), const),
                pl.BlockSpec((RANK_PAD, GLA_DK), const),
                pl.BlockSpec((1, GLA_DK), const),
                pl.BlockSpec((1, GLA_HV), const)]
    out_widths = [GLA_DV, SB_WIDTH, SB_WIDTH, 2 * D_MODEL]
    out_specs = [pl.BlockSpec((tm, n), lambda s, src_ref: (tile_of(s), 0)) for n in out_widths]
    out_specs[1] = pl.BlockSpec(
        (2 * tm, SB_WIDTH), lambda s, src_ref: (jnp.maximum(tile_of(s) - 1, 0) // 2, 0))
    out_shape = [jax.ShapeDtypeStruct((m, n), BF16) for n in out_widths]
    grid_spec = pltpu.PrefetchScalarGridSpec(
        num_scalar_prefetch=1,
        grid=(PACK_STEPS + m // tm,),
        in_specs=in_specs,
        out_specs=out_specs,
        scratch_shapes=[
            pltpu.VMEM((PACK_STEPS, D_MODEL, PACK_COLS), BF16),
            pltpu.VMEM((GLA_HEADS, GLA_HV, GLA_HK), F32),
            pltpu.VMEM((tm, SB_WIDTH), BF16),
            pltpu.VMEM((seq, SB_WIDTH), BF16),
            pltpu.VMEM((seq, SB_WIDTH), BF16),
            pltpu.VMEM((tm, SB_WIDTH), F32),
            pltpu.VMEM((SB_HEADS, SUBLANES, tm), F32),
        ])
    return pl.pallas_call(
        functools.partial(_mixer_kernel, tiles_per_seq=seq // tm),
        grid_spec=grid_spec,
        out_shape=out_shape,
        compiler_params=pltpu.CompilerParams(
            dimension_semantics=("arbitrary",), vmem_limit_bytes=VMEM_LIMIT),
        name="mixers",
    )(jnp.asarray(src, jnp.int32), w_in.T, x2, norm_g, b_gate, wdu, b_dec, gla_norm_g)


def _out_kernel(oa_ref, ob_ref, sg_ref, m_ref, x_ref, wpa_ref, wpb_ref, wo_ref, fg_ref, o_ref,
                wpa_b, wpb_b, wo_b):
    @pl.when(pl.program_id(0) == 0)
    def _():
        wpa_b[...] = wpa_ref[...].astype(BF16)
        wpb_b[...] = wpb_ref[...].astype(BF16)
        wo_b[...] = wo_ref[...].astype(BF16)

    tm = o_ref.shape[0]
    groups = [pl.ds(r, OUT_ROWS) for r in range(0, tm, OUT_ROWS)]

    def branches(rows):
        ya = jnp.dot(oa_ref[rows, :], wpa_b[...], preferred_element_type=F32)
        yb = jnp.dot(ob_ref[rows, :] * sg_ref[rows, :], wpb_b[...], preferred_element_type=F32)
        return ya, yb

    def merge(rows, ya, yb):
        merged = (m_ref[rows, :D_MODEL].astype(F32) * ya + m_ref[rows, D_MODEL:].astype(F32) * yb)
        return x_ref[rows, :] + jnp.dot(merged.astype(BF16), wo_b[...],
                                        preferred_element_type=F32)

    def finish(rows, y):
        o_ref[rows, :] = y * lax.rsqrt(jnp.mean(y * y, axis=-1, keepdims=True) + EPS) * fg_ref[...]

    yab = branches(groups[0])
    y_prev = None
    for g, rows in enumerate(groups):
        yab_next = branches(groups[g + 1]) if g + 1 < len(groups) else None
        y = merge(rows, *yab)
        if y_prev is not None:
            finish(groups[g - 1], y_prev)
        yab, y_prev = yab_next, y
    finish(groups[-1], y_prev)


def _out(oa, ob, sg, gates, x2, wpa, wpb, wo, final_g, tm):
    m = x2.shape[0]
    const = lambda i: (0, 0)
    rows = lambda n: pl.BlockSpec((tm, n), lambda i: (i, 0))
    wspec = pl.BlockSpec((D_MODEL, D_MODEL), const, pipeline_mode=pl.Buffered(1))
    return pl.pallas_call(
        _out_kernel,
        grid=(m // tm,),
        in_specs=[rows(GLA_DV), rows(SB_WIDTH), rows(SB_WIDTH), rows(2 * D_MODEL), rows(D_MODEL),
                  wspec, wspec, wspec, pl.BlockSpec((1, D_MODEL), const)],
        out_specs=rows(D_MODEL),
        out_shape=jax.ShapeDtypeStruct((m, D_MODEL), F32),
        scratch_shapes=[pltpu.VMEM((D_MODEL, D_MODEL), BF16)] * 3,
        compiler_params=pltpu.CompilerParams(
            dimension_semantics=("arbitrary",), vmem_limit_bytes=OUT_VMEM_LIMIT),
        name="outproj",
    )(oa, ob, sg, gates, x2, wpa, wpb, wo, final_g)


def kernel(x, norm_g, w_in, w_dec_up, b_dec, gla_norm_g, w_pa, w_pb, b_gate, w_o, final_g):
    batch, seq, d = x.shape
    tile, out_rows = GLA_GROUP * GLA_CHUNK, 1024
    assert d == D_MODEL and w_in.shape == (D_MODEL, _REF_COLS)
    assert seq % tile == 0 and (batch * seq) % out_rows == 0
    assert (batch * seq // tile) % 2 == 0
    x2 = x.reshape(batch * seq, D_MODEL)

    wdu = jnp.pad(w_dec_up, ((0, RANK_PAD - GLA_RANK), (0, 0))).astype(BF16)

    oa, ob, sg, gates = _mixers(
        x2, norm_g.reshape(1, D_MODEL), w_in, b_gate.reshape(1, 2 * D_MODEL),
        wdu, b_dec.reshape(1, GLA_DK), gla_norm_g.reshape(1, GLA_HV), seq)

    out = _out(oa, ob, sg, gates, x2, w_pa, w_pb, w_o, final_g.reshape(1, D_MODEL), tm=out_rows)
    return out.reshape(batch, seq, D_MODEL)
```
